```python
import jax, jax.numpy as jnp
from jax import lax
import numpy as np

D_MODEL = 2048
BATCH = 1
SEQ = 8192
DEPTH = 2
DEC_BATCH = 16
DEC_SEQ = 2048
PAST_LEN = 128

GRID_W = 64
HEAD_DIM = 128
A_HEADS = 8
A_KV_HEADS = 2
A_WINDOW = 128
A_BLOCK = 128
B_HEADS = 4
B_DK = 128
B_DV = 256
B_CHUNK = 64
B_GATE_RANK = 16
B_GATE_TAU = 16.0
C_HEADS = 8
C_WIN_R = 8
C_WIN_C = 16
D_FF = -(-8 * D_MODEL // (3 * 256)) * 256
RMS_EPS = 1e-6
NEG_INF = -1e30

A_Q = A_HEADS * HEAD_DIM
A_KV = A_KV_HEADS * HEAD_DIM
B_QK = B_HEADS * B_DK
B_V = B_HEADS * B_DV
C_W = C_HEADS * HEAD_DIM
SPLIT_SIZES = (A_Q, A_KV, A_KV, B_QK, B_QK, B_V, B_V, 2 * B_GATE_RANK, C_W, C_W, C_W, 3 * D_MODEL)
SPLIT_IDX = tuple(int(i) for i in np.cumsum(SPLIT_SIZES)[:-1])
IN_COLS = int(sum(SPLIT_SIZES))

kernel_name = 'hybrid_bidir_encoder_gqa_gla_natten'


def _rmsnorm(x, g):
    x32 = x.astype(jnp.float32)
    y = x32 * lax.rsqrt(jnp.mean(x32 * x32, axis=-1, keepdims=True) + RMS_EPS)
    return (y * g.astype(jnp.float32)).astype(x.dtype)


def _window_gqa(q, k, v, sink):
    B, S = q.shape[0], q.shape[1]
    nb = S // A_BLOCK
    G = A_HEADS // A_KV_HEADS
    qb = q.reshape(B, nb, A_BLOCK, A_KV_HEADS, G, HEAD_DIM)

    def band(t):
        tp = jnp.pad(t, ((0, 0), (A_BLOCK, A_BLOCK), (0, 0), (0, 0)))
        tp = tp.reshape(B, nb + 2, A_BLOCK, A_KV_HEADS, HEAD_DIM)
        return jnp.concatenate([tp[:, :-2], tp[:, 1:-1], tp[:, 2:]], axis=2)

    kb, vb = band(k), band(v)
    s = jnp.einsum('bnqhgd,bnkhd->bnhgqk', qb, kb, preferred_element_type=jnp.float32).astype(jnp.float32)
    s = s * (HEAD_DIM ** -0.5)
    qpos = jnp.arange(A_BLOCK)[:, None]
    kpos = jnp.arange(3 * A_BLOCK)[None, :] - A_BLOCK
    dist = jnp.abs(qpos - kpos)
    kabs = jnp.arange(nb)[:, None, None] * A_BLOCK + kpos[None]
    valid = (dist <= A_WINDOW)[None] & (kabs >= 0) & (kabs < S)
    slopes = jnp.exp2(-8.0 * jnp.arange(1, A_HEADS + 1, dtype=jnp.float32) / A_HEADS).reshape(A_KV_HEADS, G)
    s = s - slopes[:, :, None, None] * dist.astype(jnp.float32)
    s = jnp.where(valid[None, :, None, None], s, NEG_INF)
    sink_l = jnp.broadcast_to(sink.astype(jnp.float32).reshape(A_KV_HEADS, G, 1, 1), s.shape[:-1] + (1,))
    p = jax.nn.softmax(jnp.concatenate([s, sink_l], axis=-1), axis=-1)[..., :-1]
    o = jnp.einsum('bnhgqk,bnkhd->bnqhgd', p.astype(v.dtype), vb)
    return o.reshape(B, S, A_Q)


def _gla_causal(q, k, v, lg):
    B, S, H, K = q.shape
    V = v.shape[-1]
    n = S // B_CHUNK
    f32 = jnp.float32
    q, k, v, lg = [t.astype(f32).reshape(B, n, B_CHUNK, H, t.shape[-1]) for t in (q, k, v, lg)]
    b = jnp.cumsum(lg, axis=2)
    b_last = b[:, :, -1:]
    qt = q * jnp.exp(b)
    kt = k * jnp.exp(-b)
    kd = k * jnp.exp(b_last - b)
    tri = jnp.tril(jnp.ones((B_CHUNK, B_CHUNK), dtype=bool))
    a = jnp.where(tri, jnp.einsum('bnqhk,bnshk->bnhqs', qt, kt), 0.0)
    o = jnp.einsum('bnhqs,bnshv->bnqhv', a, v)
    u = jnp.einsum('bnshk,bnshv->bnhkv', kd, v)
    decay = jnp.exp(b_last[:, :, 0])

    def step(state, inp):
        d, uu = inp
        return d[..., None] * state + uu, state

    _, s_prev = lax.scan(step, jnp.zeros((B, H, K, V), f32), (jnp.moveaxis(decay, 1, 0), jnp.moveaxis(u, 1, 0)))
    s_prev = jnp.moveaxis(s_prev, 0, 1)
    o = o + jnp.einsum('bnqhk,bnhkv->bnqhv', qt, s_prev)
    return o.reshape(B, S, H, V)


def _gla_branch(q, k, v, og, gr, w2, bias, gain):
    B, S = q.shape[0], q.shape[1]
    q = q.reshape(B, S, B_HEADS, B_DK) * (B_DK ** -0.5)
    k = k.reshape(B, S, B_HEADS, B_DK)
    v = v.reshape(B, S, B_HEADS, B_DV)
    gr = gr.reshape(B, S, 2, B_GATE_RANK)
    logits = jnp.einsum('bsdr,drk->bsdk', gr, w2) + bias
    lg = (jax.nn.log_sigmoid(logits.astype(jnp.float32)) / B_GATE_TAU).reshape(B, S, 2, B_HEADS, B_DK)
    fwd = _gla_causal(q, k, v, lg[:, :, 0])
    flip = lambda t: jnp.flip(t, axis=1)
    bwd = flip(_gla_causal(flip(q), flip(k), flip(v), flip(lg[:, :, 1])))
    o = fwd + bwd
    o = o * lax.rsqrt(jnp.mean(o * o, axis=-1, keepdims=True) + RMS_EPS) * gain.astype(jnp.float32)
    return o.reshape(B, S, B_V).astype(og.dtype) * jax.nn.silu(og)


def _neighbourhood_attn(q, k, v, rpb):
    B, S = q.shape[0], q.shape[1]
    R = S // GRID_W
    kr = min(C_WIN_R, R)
    qg = q.reshape(B, R, GRID_W, C_HEADS, HEAD_DIM)
    kg = k.reshape(B, R, GRID_W, C_HEADS, HEAD_DIM)
    vg = v.reshape(B, R, GRID_W, C_HEADS, HEAD_DIM)
    rows = jnp.arange(R)
    row_idx = jnp.clip(rows - kr // 2, 0, R - kr)[:, None] + jnp.arange(kr)[None, :]
    kband = jnp.take(kg, row_idx, axis=1)
    vband = jnp.take(vg, row_idx, axis=1)
    cols = jnp.arange(GRID_W)
    col_start = jnp.clip(cols - C_WIN_C // 2, 0, GRID_W - C_WIN_C)
    in_win = (cols[None, :] >= col_start[:, None]) & (cols[None, :] < col_start[:, None] + C_WIN_C)
    dr_idx = row_idx - rows[:, None] + (C_WIN_R - 1)
    dc_idx = jnp.clip(cols[None, :] - cols[:, None], -(C_WIN_C - 1), C_WIN_C - 1) + (C_WIN_C - 1)
    bias = rpb.astype(jnp.float32)[:, dr_idx][:, :, :, dc_idx]
    bias = jnp.where(in_win[None, None, None], bias, NEG_INF).transpose(0, 1, 3, 2, 4)
    s = jnp.einsum('brqhd,brikhd->bhrqik', qg, kband, preferred_element_type=jnp.float32).astype(jnp.float32)
    s = s * (HEAD_DIM ** -0.5) + bias[None]
    p = jax.nn.softmax(s, axis=(-2, -1))
    o = jnp.einsum('bhrqik,brikhd->brqhd', p.astype(v.dtype), vband)
    return o.reshape(B, S, C_W)


def _mixer(h, w_in, sink_a, gla_w2, gla_b, gla_norm, rpb_c, w_br_a, w_br_b, w_br_c, w_out):
    B, S = h.shape[0], h.shape[1]
    proj = h @ w_in
    (a_q, a_k, a_v, b_q, b_k, b_v, b_og, b_gr, c_q, c_k, c_v, gl) = jnp.split(proj, SPLIT_IDX, axis=-1)
    o_a = _window_gqa(a_q.reshape(B, S, A_HEADS, HEAD_DIM), a_k.reshape(B, S, A_KV_HEADS, HEAD_DIM),
                      a_v.reshape(B, S, A_KV_HEADS, HEAD_DIM), sink_a)
    o_b = _gla_branch(b_q, b_k, b_v, b_og, b_gr, gla_w2, gla_b, gla_norm)
    o_c = _neighbourhood_attn(c_q.reshape(B, S, C_HEADS, HEAD_DIM), c_k.reshape(B, S, C_HEADS, HEAD_DIM),
                              c_v.reshape(B, S, C_HEADS, HEAD_DIM), rpb_c)
    g_a, g_b, g_c = jnp.split(jax.nn.sigmoid(gl), 3, axis=-1)
    merged = g_a * (o_a @ w_br_a) + g_b * (o_b @ w_br_b) + g_c * (o_c @ w_br_c)
    return merged @ w_out


def _trunk(x, norm1, w_in, sink_a, gla_w2, gla_b, gla_norm, rpb_c, w_br_a, w_br_b, w_br_c, w_out,
           norm2, w_ffn_in, w_ffn_out, norm_f):
    for l in range(DEPTH):
        h = _rmsnorm(x, norm1[l])
        x = x + _mixer(h, w_in[l], sink_a[l], gla_w2[l], gla_b[l], gla_norm[l], rpb_c[l],
                       w_br_a[l], w_br_b[l], w_br_c[l], w_out[l])
        h = _rmsnorm(x, norm2[l])
        g, u = jnp.split(h @ w_ffn_in[l], 2, axis=-1)
        x = x + (jax.nn.silu(g) * u) @ w_ffn_out[l]
    return _rmsnorm(x, norm_f)


def setup_inputs(seed: int = 0) -> dict:
    key = jax.random.key(seed)
    ks = jax.random.split(key, 20)
    f32 = jnp.float32
    n = lambda k, shape: jax.random.normal(k, shape, f32)
    return {
        'x_prompt': n(ks[0], (BATCH, SEQ, D_MODEL)),
        'x_sample': n(ks[1], (DEC_BATCH, DEC_SEQ, D_MODEL)),
        'norm1': 1.0 + 0.02 * n(ks[2], (DEPTH, D_MODEL)),
        'w_in': n(ks[3], (DEPTH, D_MODEL, IN_COLS)) * D_MODEL ** -0.5,
        'sink_a': 0.5 * n(ks[4], (DEPTH, A_HEADS)),
        'gla_w2': n(ks[5], (DEPTH, 2, B_GATE_RANK, B_QK)) * B_GATE_RANK ** -0.5,
        'gla_b': 0.1 * n(ks[6], (DEPTH, 2, B_QK)),
        'gla_norm': 1.0 + 0.02 * n(ks[7], (DEPTH, B_DV)),
        'rpb_c': 0.1 * n(ks[8], (DEPTH, C_HEADS, 2 * C_WIN_R - 1, 2 * C_WIN_C - 1)),
        'w_br_a': n(ks[9], (DEPTH, A_Q, D_MODEL)) * A_Q ** -0.5,
        'w_br_b': n(ks[10], (DEPTH, B_V, D_MODEL)) * B_V ** -0.5,
        'w_br_c': n(ks[11], (DEPTH, C_W, D_MODEL)) * C_W ** -0.5,
        'w_out': n(ks[12], (DEPTH, D_MODEL, D_MODEL)) * D_MODEL ** -0.5,
        'norm2': 1.0 + 0.02 * n(ks[13], (DEPTH, D_MODEL)),
        'w_ffn_in': n(ks[14], (DEPTH, D_MODEL, 2 * D_FF)) * D_MODEL ** -0.5,
        'w_ffn_out': n(ks[15], (DEPTH, D_FF, D_MODEL)) * D_FF ** -0.5,
        'norm_f': 1.0 + 0.02 * n(ks[16], (D_MODEL,)),
    }


def reference(x_prompt, x_sample, norm1, w_in, sink_a, gla_w2, gla_b, gla_norm, rpb_c, w_br_a, w_br_b,
              w_br_c, w_out, norm2, w_ffn_in, w_ffn_out, norm_f):
    y_prompt = _trunk(x_prompt, norm1, w_in, sink_a, gla_w2, gla_b, gla_norm, rpb_c, w_br_a, w_br_b, w_br_c,
                      w_out, norm2, w_ffn_in, w_ffn_out, norm_f)
    y_sample = _trunk(x_sample, norm1, w_in, sink_a, gla_w2, gla_b, gla_norm, rpb_c, w_br_a, w_br_b, w_br_c,
                      w_out, norm2, w_ffn_in, w_ffn_out, norm_f)
    return (y_prompt, y_sample)
```

```python
import functools

import numpy as np
import jax
import jax.numpy as jnp
from jax import lax
from jax.experimental import pallas as pl
from jax.experimental.pallas import tpu as pltpu

F32 = jnp.float32
BF16 = jnp.bfloat16

D_MODEL = 2048
DEPTH = 2
GRID_W = 64
HEAD_DIM = 128
A_HEADS = 8
A_KV_HEADS = 2
A_GROUP = A_HEADS // A_KV_HEADS
A_WINDOW = 128
A_BLOCK = 128
B_HEADS = 4
B_DK = 128
B_DV = 256
B_CHUNK = 64
B_GATE_RANK = 16
B_GATE_TAU = 16.0
C_HEADS = 8
C_WIN_R = 8
C_WIN_C = 16
D_FF = -(-8 * D_MODEL // (3 * 256)) * 256
RMS_EPS = 1e-6
NEG_INF = -1e30

A_Q = A_HEADS * HEAD_DIM
A_KV = A_KV_HEADS * HEAD_DIM
B_QK = B_HEADS * B_DK
B_V = B_HEADS * B_DV
C_W = C_HEADS * HEAD_DIM
SPLIT_SIZES = (A_Q, A_KV, A_KV, B_QK, B_QK, B_V, B_V, 2 * B_GATE_RANK, C_W, C_W, C_W, 3 * D_MODEL)
SPLIT_OFF = tuple(int(i) for i in np.cumsum((0,) + SPLIT_SIZES))
(IN_A_Q, IN_A_K, IN_A_V, IN_B_Q, IN_B_K, IN_B_V, IN_B_OG, IN_B_GR, IN_C_Q, IN_C_K, IN_C_V, IN_GL) = SPLIT_OFF[:-1]

P_A_Q = 0
P_B_V = 1024
P_B_OG = 2048
P_C_Q = 3072
P_C_K = 4096
P_C_V = 5120
P_GATE = 6144
P_B_Q = 12288
P_B_K = 12800
P_A_K = 13312
P_A_V = 13568
P_COLS = 13824
GR_PAD = 128

TILE = 2048
NA_HALO = 512
NA_PAIR_KEYS = 10 * GRID_W

VMEM_LIMIT = 56 * 1024 * 1024


def _cparams(sem):
    return pltpu.CompilerParams(dimension_semantics=sem, vmem_limit_bytes=VMEM_LIMIT)


def _seq_bounds(pos, seqs):
    tp, sp, ss = seqs
    in_p = pos < tp
    lo_p = lax.div(pos, sp) * sp
    lo_s = tp + lax.div(jnp.maximum(pos - tp, 0), ss) * ss
    lo = jnp.where(in_p, lo_p, lo_s)
    hi = lo + jnp.where(in_p, sp, ss)
    return lo, hi


NORM_ROWS = 256


def _rmsnorm_rows(x_ref, g_ref, h_ref):
    def body(c, carry):
        r = pl.multiple_of(c * NORM_ROWS, NORM_ROWS)
        xs = x_ref[pl.ds(r, NORM_ROWS), :]
        ms = jnp.mean(xs * xs, axis=-1, keepdims=True)
        h_ref[pl.ds(r, NORM_ROWS), :] = (xs * lax.rsqrt(ms + RMS_EPS) * g_ref[...]).astype(h_ref.dtype)
        return carry
    lax.fori_loop(0, x_ref.shape[0] // NORM_ROWS, body, 0)


def _inproj_kernel(x_ref, g_ref, w_ref, wgr_ref, proj_ref, gr_ref, h_ref):
    @pl.when(pl.program_id(1) == 0)
    def _():
        _rmsnorm_rows(x_ref, g_ref, h_ref)
        gr_ref[...] = jnp.dot(h_ref[...], wgr_ref[...], preferred_element_type=F32)
    proj_ref[...] = jnp.dot(h_ref[...], w_ref[...], preferred_element_type=F32).astype(proj_ref.dtype)


def _inproj(x, g, w, wgr, bm=1024, bn=1536):
    t, d = x.shape
    n = w.shape[1]
    return pl.pallas_call(
        _inproj_kernel,
        grid=(t // bm, n // bn),
        in_specs=[
            pl.BlockSpec((bm, d), lambda i, j: (i, 0)),
            pl.BlockSpec((1, d), lambda i, j: (0, 0)),
            pl.BlockSpec((d, bn), lambda i, j: (0, j)),
            pl.BlockSpec((d, GR_PAD), lambda i, j: (0, 0)),
        ],
        out_specs=[
            pl.BlockSpec((bm, bn), lambda i, j: (i, j)),
            pl.BlockSpec((bm, GR_PAD), lambda i, j: (i, 0)),
        ],
        out_shape=[jax.ShapeDtypeStruct((t, n), BF16), jax.ShapeDtypeStruct((t, GR_PAD), F32)],
        scratch_shapes=[pltpu.VMEM((bm, d), BF16)],
        compiler_params=_cparams(("parallel", "arbitrary")),
        name="inproj",
    )(x, g, w, wgr)


def _ffn_in_kernel(x_ref, g_ref, w_ref, act_ref, h_ref):
    @pl.when(pl.program_id(1) == 0)
    def _():
        _rmsnorm_rows(x_ref, g_ref, h_ref)
    r = jnp.dot(h_ref[...], w_ref[...], preferred_element_type=F32)
    half = r.shape[1] // 2
    gate = r[:, :half]
    act_ref[...] = (gate * jax.nn.sigmoid(gate) * r[:, half:]).astype(act_ref.dtype)


def _ffn_in(x, g, w, bm=1024, bn=512):
    t, d = x.shape
    ff = w.shape[1] // 2
    return pl.pallas_call(
        _ffn_in_kernel,
        grid=(t // bm, ff // bn),
        in_specs=[
            pl.BlockSpec((bm, d), lambda i, j: (i, 0)),
            pl.BlockSpec((1, d), lambda i, j: (0, 0)),
            pl.BlockSpec((d, 2 * bn), lambda i, j: (0, j)),
        ],
        out_specs=pl.BlockSpec((bm, bn), lambda i, j: (i, j)),
        out_shape=jax.ShapeDtypeStruct((t, ff), BF16),
        scratch_shapes=[pltpu.VMEM((bm, d), BF16)],
        compiler_params=_cparams(("parallel", "arbitrary")),
        name="ffn_in",
    )(x, g, w)


def _mm_res_kernel(a_ref, w_ref, x_ref, o_ref):
    o_ref[...] = x_ref[...] + jnp.dot(a_ref[...], w_ref[...], preferred_element_type=F32)


def _mm_res(a, w, x, bm, bn, name):
    t, k = a.shape
    n = w.shape[1]
    return pl.pallas_call(
        _mm_res_kernel,
        grid=(t // bm, n // bn),
        in_specs=[
            pl.BlockSpec((bm, k), lambda i, j: (i, 0)),
            pl.BlockSpec((k, bn), lambda i, j: (0, j)),
            pl.BlockSpec((bm, bn), lambda i, j: (i, j)),
        ],
        out_specs=pl.BlockSpec((bm, bn), lambda i, j: (i, j)),
        out_shape=jax.ShapeDtypeStruct((t, n), F32),
        compiler_params=_cparams(("parallel", "arbitrary")),
        name=name,
    )(a, w, x)


MERGE_ROWS = 256


def _merge_kernel(oa_ref, oc_ref, of_ref, obw_ref, og_ref, gain_ref, ga_ref, gb_ref, gc_ref,
                  wa_ref, wb_ref, wc_ref, out_ref, ob_ref):
    @pl.when(pl.program_id(1) == 0)
    def _():
        def body(c, carry):
            r = pl.multiple_of(c * MERGE_ROWS, MERGE_ROWS)
            o = of_ref[pl.ds(r, MERGE_ROWS), :].astype(F32) + obw_ref[pl.ds(r, MERGE_ROWS), :].astype(F32)
            og = og_ref[pl.ds(r, MERGE_ROWS), :].astype(F32)
            for h in range(B_HEADS):
                sl = slice(h * B_DV, (h + 1) * B_DV)
                oh = o[:, sl]
                ms = jnp.mean(oh * oh, axis=-1, keepdims=True)
                ogh = og[:, sl]
                y = oh * lax.rsqrt(ms + RMS_EPS) * gain_ref[...] * (ogh * jax.nn.sigmoid(ogh))
                ob_ref[pl.ds(r, MERGE_ROWS), sl] = y.astype(ob_ref.dtype)
            return carry
        lax.fori_loop(0, of_ref.shape[0] // MERGE_ROWS, body, 0)

    ya = jnp.dot(oa_ref[...], wa_ref[...], preferred_element_type=F32)
    acc = jax.nn.sigmoid(ga_ref[...].astype(F32)) * ya
    yb = jnp.dot(ob_ref[...], wb_ref[...], preferred_element_type=F32)
    acc = acc + jax.nn.sigmoid(gb_ref[...].astype(F32)) * yb
    yc = jnp.dot(oc_ref[...], wc_ref[...], preferred_element_type=F32)
    acc = acc + jax.nn.sigmoid(gc_ref[...].astype(F32)) * yc
    out_ref[...] = acc.astype(out_ref.dtype)


def _merge(o_a, o_c, o_f, o_bw, proj, gain, wa, wb, wc, bm=1024, bn=512):
    t = o_a.shape[0]
    d = wa.shape[1]
    kb = o_a.shape[1]
    gate0 = P_GATE // bn
    gstep = D_MODEL // bn
    wide = lambda i, j: (i, 0)
    return pl.pallas_call(
        _merge_kernel,
        grid=(t // bm, d // bn),
        in_specs=[
            pl.BlockSpec((bm, kb), wide),
            pl.BlockSpec((bm, kb), wide),
            pl.BlockSpec((bm, kb), wide),
            pl.BlockSpec((bm, kb), wide),
            pl.BlockSpec((bm, B_V), lambda i, j: (i, P_B_OG // B_V)),
            pl.BlockSpec((1, B_DV), lambda i, j: (0, 0)),
            pl.BlockSpec((bm, bn), lambda i, j: (i, gate0 + j)),
            pl.BlockSpec((bm, bn), lambda i, j: (i, gate0 + gstep + j)),
            pl.BlockSpec((bm, bn), lambda i, j: (i, gate0 + 2 * gstep + j)),
            pl.BlockSpec((kb, bn), lambda i, j: (0, j)),
            pl.BlockSpec((kb, bn), lambda i, j: (0, j)),
            pl.BlockSpec((kb, bn), lambda i, j: (0, j)),
        ],
        out_specs=pl.BlockSpec((bm, bn), lambda i, j: (i, j)),
        out_shape=jax.ShapeDtypeStruct((t, d), BF16),
        scratch_shapes=[pltpu.VMEM((bm, kb), BF16)],
        compiler_params=_cparams(("parallel", "arbitrary")),
        name="merge",
    )(o_a, o_c, o_f, o_bw, proj, gain, proj, proj, proj, wa, wb, wc)


def _final_norm_kernel(x_ref, g_ref, y_ref):
    _rmsnorm_rows(x_ref, g_ref, y_ref)


def _final_norm(x, g, row0, rows, bm=512):
    d = x.shape[1]
    blk0 = row0 // bm
    return pl.pallas_call(
        _final_norm_kernel,
        grid=(rows // bm,),
        in_specs=[
            pl.BlockSpec((bm, d), lambda i: (blk0 + i, 0)),
            pl.BlockSpec((1, d), lambda i: (0, 0)),
        ],
        out_specs=pl.BlockSpec((bm, d), lambda i: (i, 0)),
        out_shape=jax.ShapeDtypeStruct((rows, d), F32),
        compiler_params=_cparams(("parallel",)),
        name="final_norm",
    )(x, g)


A_KEYS = 3 * A_BLOCK
A_ROWS = A_GROUP * A_BLOCK


def _nt_dot(a, b):
    return lax.dot_general(a, b, (((1,), (1,)), ((), ())), preferred_element_type=F32)


def _win_kernel(seqs, sink_ref, q_ref, kp_ref, kc_ref, kn_ref, vp_ref, vc_ref, vn_ref, o_ref, kbuf, vbuf):
    tile_start = pl.program_id(0) * TILE
    seq_lo, seq_hi = _seq_bounds(tile_start, seqs)
    kbuf[0:A_BLOCK, :] = kp_ref[...]
    kbuf[A_BLOCK:A_BLOCK + TILE, :] = kc_ref[...]
    kbuf[A_BLOCK + TILE:, :] = kn_ref[...]
    vbuf[0:A_BLOCK, :] = vp_ref[...]
    vbuf[A_BLOCK:A_BLOCK + TILE, :] = vc_ref[...]
    vbuf[A_BLOCK + TILE:, :] = vn_ref[...]

    row = lax.broadcasted_iota(jnp.int32, (A_ROWS, A_KEYS), 0)
    col = lax.broadcasted_iota(jnp.int32, (A_ROWS, A_KEYS), 1)
    dist = jnp.abs((row % A_BLOCK) - (col - A_BLOCK))
    in_window = dist <= A_WINDOW
    distf = dist.astype(F32)
    grp = lax.broadcasted_iota(jnp.int32, (A_ROWS, 1), 0) // A_BLOCK
    kcol = lax.broadcasted_iota(jnp.int32, (1, A_KEYS), 1)
    scale = HEAD_DIM ** -0.5

    for kvh in range(A_KV_HEADS):
        slope = jnp.zeros((A_ROWS, 1), F32)
        sink = jnp.zeros((A_ROWS, 1), F32)
        for g in range(A_GROUP):
            h = kvh * A_GROUP + g
            slope = jnp.where(grp == g, 2.0 ** (-8.0 * (h + 1) / A_HEADS), slope)
            sink = jnp.where(grp == g, sink_ref[h], sink)
        nbias = -slope * distf
        ksl = slice(kvh * HEAD_DIM, (kvh + 1) * HEAD_DIM)

        def body(n, carry):
            r = pl.multiple_of(n * A_BLOCK, A_BLOCK)
            q4 = jnp.concatenate(
                [q_ref[pl.ds(r, A_BLOCK), (kvh * A_GROUP + g) * HEAD_DIM:(kvh * A_GROUP + g + 1) * HEAD_DIM]
                 for g in range(A_GROUP)], axis=0)
            kw = kbuf[pl.ds(r, A_KEYS), ksl]
            vw = vbuf[pl.ds(r, A_KEYS), ksl]
            s = _nt_dot(q4, kw) * scale + nbias
            kabs = tile_start - A_BLOCK + r + kcol
            valid = in_window & (kabs >= seq_lo) & (kabs < seq_hi)
            s = jnp.where(valid, s, NEG_INF)
            m = jnp.maximum(jnp.max(s, axis=-1, keepdims=True), sink)
            p = jnp.exp(s - m)
            den = jnp.sum(p, axis=-1, keepdims=True) + jnp.exp(sink - m)
            o = jnp.dot(p.astype(BF16), vw, preferred_element_type=F32) / den
            for g in range(A_GROUP):
                h = kvh * A_GROUP + g
                o_ref[pl.ds(r, A_BLOCK), h * HEAD_DIM:(h + 1) * HEAD_DIM] = (
                    o[g * A_BLOCK:(g + 1) * A_BLOCK].astype(o_ref.dtype))
            return carry
        lax.fori_loop(0, TILE // A_BLOCK, body, 0)


def _win_attn(proj, sink, seqs):
    t = proj.shape[0]
    nt = t // TILE
    per = TILE // A_BLOCK
    last = t // A_BLOCK - 1
    kblk = P_A_K // A_KV
    vblk = P_A_V // A_KV
    prev = lambda col: (lambda g: (jnp.maximum(g * per - 1, 0), col))
    cur = lambda col: (lambda g: (g, col))
    nxt = lambda col: (lambda g: (jnp.minimum((g + 1) * per, last), col))
    return pl.pallas_call(
        functools.partial(_win_kernel, seqs),
        grid=(nt,),
        in_specs=[
            pl.BlockSpec(memory_space=pltpu.SMEM),
            pl.BlockSpec((TILE, A_Q), lambda g: (g, P_A_Q // A_Q)),
            pl.BlockSpec((A_BLOCK, A_KV), prev(kblk)),
            pl.BlockSpec((TILE, A_KV), cur(kblk)),
            pl.BlockSpec((A_BLOCK, A_KV), nxt(kblk)),
            pl.BlockSpec((A_BLOCK, A_KV), prev(vblk)),
            pl.BlockSpec((TILE, A_KV), cur(vblk)),
            pl.BlockSpec((A_BLOCK, A_KV), nxt(vblk)),
        ],
        out_specs=pl.BlockSpec((TILE, A_Q), lambda g: (g, 0)),
        out_shape=jax.ShapeDtypeStruct((t, A_Q), BF16),
        scratch_shapes=[pltpu.VMEM((TILE + 2 * A_BLOCK, A_KV), BF16),
                        pltpu.VMEM((TILE + 2 * A_BLOCK, A_KV), BF16)],
        compiler_params=_cparams(("parallel",)),
        name="win_attn",
    )(sink, proj, proj, proj, proj, proj, proj, proj)


def _gla_chunk(q_ref, k_ref, v_ref, gr_ref, w2_ref, bias, r, tri, tri_bf, causal, o_ref, s_ref):
    gr = gr_ref[pl.ds(r, B_CHUNK), :].astype(BF16)
    logits = jnp.dot(gr, w2_ref[...], preferred_element_type=F32) + bias
    lg = (jnp.minimum(logits, 0.0) - jnp.log(1.0 + jnp.exp(-jnp.abs(logits)))) * (1.0 / B_GATE_TAU)
    hi = lg.astype(BF16)
    lo = (lg - hi.astype(F32)).astype(BF16)
    b = jnp.dot(tri_bf, hi, preferred_element_type=F32) + jnp.dot(tri_bf, lo, preferred_element_type=F32)
    tot = b[B_CHUNK - 1:B_CHUNK, :] if causal else b[0:1, :]
    q = q_ref[pl.ds(r, B_CHUNK), :].astype(F32) * (B_DK ** -0.5)
    k = k_ref[pl.ds(r, B_CHUNK), :].astype(F32)
    qt = (q * jnp.exp(b)).astype(BF16)
    kt = (k * jnp.exp(-b)).astype(BF16)
    kd = k * jnp.exp(tot - b)
    v = v_ref[pl.ds(r, B_CHUNK), :]
    for h in range(B_HEADS):
        ks = slice(h * B_DK, (h + 1) * B_DK)
        vs = slice(h * B_DV, (h + 1) * B_DV)
        qh = qt[:, ks]
        vh = v[:, vs]
        a = jnp.where(tri, _nt_dot(qh, kt[:, ks]), 0.0).astype(BF16)
        state = s_ref[h]
        o = jnp.dot(a, vh, preferred_element_type=F32) + jnp.dot(qh, state.astype(BF16), preferred_element_type=F32)
        o_ref[pl.ds(r, B_CHUNK), vs] = o.astype(o_ref.dtype)
        kdt = jnp.transpose(kd[:, ks]).astype(BF16)
        u = jnp.dot(kdt, vh, preferred_element_type=F32)
        dec = jnp.exp(jnp.transpose(jnp.broadcast_to(tot[:, ks], (B_DK, B_DK))))
        s_ref[h] = jnp.concatenate([dec] * (B_DV // B_DK), axis=1) * state + u


def _gla_kernel(seqs, qf_ref, kf_ref, vf_ref, grf_ref, qb_ref, kb_ref, vb_ref, grb_ref, w2f_ref, w2b_ref,
                bias_ref, of_ref, ob_ref, sf_ref, sb_ref):
    i = pl.program_id(0)
    nt = pl.num_programs(0)
    f_start = i * TILE
    b_start = (nt - 1 - i) * TILE
    f_lo, _ = _seq_bounds(f_start, seqs)
    _, b_hi = _seq_bounds(b_start, seqs)

    @pl.when(f_start == f_lo)
    def _():
        sf_ref[...] = jnp.zeros_like(sf_ref)

    @pl.when(b_start + TILE == b_hi)
    def _():
        sb_ref[...] = jnp.zeros_like(sb_ref)

    row = lax.broadcasted_iota(jnp.int32, (B_CHUNK, B_CHUNK), 0)
    col = lax.broadcasted_iota(jnp.int32, (B_CHUNK, B_CHUNK), 1)
    lower = row >= col
    upper = row <= col
    lower_bf = lower.astype(BF16)
    upper_bf = upper.astype(BF16)
    bias_f = bias_ref[0:1, :]
    bias_b = bias_ref[1:2, :]
    nchunk = TILE // B_CHUNK

    def body(c, carry):
        rf = pl.multiple_of(c * B_CHUNK, B_CHUNK)
        _gla_chunk(qf_ref, kf_ref, vf_ref, grf_ref, w2f_ref, bias_f, rf, lower, lower_bf, True, of_ref, sf_ref)
        rb = pl.multiple_of((nchunk - 1 - c) * B_CHUNK, B_CHUNK)
        _gla_chunk(qb_ref, kb_ref, vb_ref, grb_ref, w2b_ref, bias_b, rb, upper, upper_bf, False, ob_ref, sb_ref)
        return carry
    lax.fori_loop(0, nchunk, body, 0)


def _gla(proj, gr, w2f, w2b, bias, seqs):
    t = proj.shape[0]
    nt = t // TILE
    fwd = lambda col: (lambda i: (i, col))
    bwd = lambda col: (lambda i: (nt - 1 - i, col))
    qblk, kblk, vblk = P_B_Q // B_QK, P_B_K // B_QK, P_B_V // B_V

    def specs(mk):
        return [pl.BlockSpec((TILE, B_QK), mk(qblk)), pl.BlockSpec((TILE, B_QK), mk(kblk)),
                pl.BlockSpec((TILE, B_V), mk(vblk)), pl.BlockSpec((TILE, GR_PAD), mk(0))]

    const = lambda i: (0, 0)
    return pl.pallas_call(
        functools.partial(_gla_kernel, seqs),
        grid=(nt,),
        in_specs=specs(fwd) + specs(bwd) + [
            pl.BlockSpec((GR_PAD, B_QK), const), pl.BlockSpec((GR_PAD, B_QK), const),
            pl.BlockSpec((2, B_QK), const)],
        out_specs=[pl.BlockSpec((TILE, B_V), fwd(0)), pl.BlockSpec((TILE, B_V), bwd(0))],
        out_shape=[jax.ShapeDtypeStruct((t, B_V), BF16), jax.ShapeDtypeStruct((t, B_V), BF16)],
        scratch_shapes=[pltpu.VMEM((B_HEADS, B_DK, B_DV), F32), pltpu.VMEM((B_HEADS, B_DK, B_DV), F32)],
        compiler_params=_cparams(("arbitrary",)),
        name="gla",
    )(proj, proj, proj, gr, proj, proj, proj, gr, w2f, w2b, bias)


NA_PATTERNS = 5
_NA_ROWS_FOR_TABLE = 32
_NA_PATTERN_ROWS = (0, 2, 8, _NA_ROWS_FOR_TABLE - 4, _NA_ROWS_FOR_TABLE - 2)


def _na_bias_table(rpb):
    R = _NA_ROWS_FOR_TABLE
    kr = C_WIN_R
    jj = np.arange(2)[:, None, None, None]
    cq = np.arange(GRID_W)[None, :, None, None]
    ki = np.arange(NA_PAIR_KEYS // GRID_W)[None, None, :, None]
    ck = np.arange(GRID_W)[None, None, None, :]
    dr_list, dc_list, ok_list = [], [], []
    for r in _NA_PATTERN_ROWS:
        start = int(np.clip(r - 4, 0, R - NA_PAIR_KEYS // GRID_W))
        qrow = r + jj
        ws = np.clip(qrow - kr // 2, 0, R - kr)
        krow = start + ki
        row_ok = (krow >= ws) & (krow < ws + kr)
        col_start = np.clip(cq - C_WIN_C // 2, 0, GRID_W - C_WIN_C)
        col_ok = (ck >= col_start) & (ck < col_start + C_WIN_C)
        dr = np.clip(krow - qrow + (C_WIN_R - 1), 0, 2 * C_WIN_R - 2)
        dc = np.clip(ck - cq, -(C_WIN_C - 1), C_WIN_C - 1) + (C_WIN_C - 1)
        shape = (2, GRID_W, NA_PAIR_KEYS // GRID_W, GRID_W)
        dr_list.append(np.broadcast_to(dr, shape))
        dc_list.append(np.broadcast_to(dc, shape))
        ok_list.append(np.broadcast_to(row_ok & col_ok, shape))
    dr = np.stack(dr_list).reshape(NA_PATTERNS, 2 * GRID_W, NA_PAIR_KEYS)
    dc = np.stack(dc_list).reshape(NA_PATTERNS, 2 * GRID_W, NA_PAIR_KEYS)
    ok = np.stack(ok_list).reshape(NA_PATTERNS, 2 * GRID_W, NA_PAIR_KEYS)
    vals = rpb.astype(F32)[:, dr, dc]
    return jnp.where(ok[None], vals, NEG_INF)


def _na_kernel(seqs, q_ref, kp_ref, kc_ref, kn_ref, vp_ref, vc_ref, vn_ref, tab_ref, o_ref, kbuf, vbuf):
    tile_start = pl.program_id(1) * TILE
    seq_lo, seq_hi = _seq_bounds(tile_start, seqs)
    r0 = lax.div(tile_start, GRID_W)
    row_lo = lax.div(seq_lo, GRID_W)
    row_hi = lax.div(seq_hi, GRID_W)
    kbuf[0:NA_HALO, :] = kp_ref[...]
    kbuf[NA_HALO:NA_HALO + TILE, :] = kc_ref[...]
    kbuf[NA_HALO + TILE:, :] = kn_ref[...]
    vbuf[0:NA_HALO, :] = vp_ref[...]
    vbuf[NA_HALO:NA_HALO + TILE, :] = vc_ref[...]
    vbuf[NA_HALO + TILE:, :] = vn_ref[...]
    scale = HEAD_DIM ** -0.5
    pair_rows = 2 * GRID_W
    key_rows = NA_PAIR_KEYS // GRID_W

    def body(pp, carry):
        r = r0 + 2 * pp
        start = jnp.clip(r - C_WIN_R // 2, row_lo, row_hi - key_rows)
        off = pl.multiple_of((start - r0 + NA_HALO // GRID_W) * GRID_W, pair_rows)
        pid = jnp.where(r == row_lo, 0,
                        jnp.where(r == row_lo + 2, 1,
                                  jnp.where(r == row_hi - 4, 3,
                                            jnp.where(r == row_hi - 2, 4, 2))))
        qs = pl.multiple_of(pp * pair_rows, pair_rows)
        q2 = q_ref[pl.ds(qs, pair_rows), :]
        kw = kbuf[pl.ds(off, NA_PAIR_KEYS), :]
        vw = vbuf[pl.ds(off, NA_PAIR_KEYS), :]
        s = _nt_dot(q2, kw) * scale + tab_ref[0, pid]
        m = jnp.max(s, axis=-1, keepdims=True)
        p = jnp.exp(s - m)
        den = jnp.sum(p, axis=-1, keepdims=True)
        o = jnp.dot(p.astype(BF16), vw, preferred_element_type=F32) / den
        o_ref[pl.ds(qs, pair_rows), :] = o.astype(o_ref.dtype)
        return carry
    lax.fori_loop(0, TILE // pair_rows, body, 0)


def _na_attn(proj, table, seqs):
    t = proj.shape[0]
    nt = t // TILE
    per = TILE // NA_HALO
    last = t // NA_HALO - 1
    qblk, kblk, vblk = P_C_Q // HEAD_DIM, P_C_K // HEAD_DIM, P_C_V // HEAD_DIM
    prev = lambda c0: (lambda h, g: (jnp.maximum(g * per - 1, 0), c0 + h))
    cur = lambda c0: (lambda h, g: (g, c0 + h))
    nxt = lambda c0: (lambda h, g: (jnp.minimum((g + 1) * per, last), c0 + h))
    return pl.pallas_call(
        functools.partial(_na_kernel, seqs),
        grid=(C_HEADS, nt),
        in_specs=[
            pl.BlockSpec((TILE, HEAD_DIM), cur(qblk)),
            pl.BlockSpec((NA_HALO, HEAD_DIM), prev(kblk)),
            pl.BlockSpec((TILE, HEAD_DIM), cur(kblk)),
            pl.BlockSpec((NA_HALO, HEAD_DIM), nxt(kblk)),
            pl.BlockSpec((NA_HALO, HEAD_DIM), prev(vblk)),
            pl.BlockSpec((TILE, HEAD_DIM), cur(vblk)),
            pl.BlockSpec((NA_HALO, HEAD_DIM), nxt(vblk)),
            pl.BlockSpec((1, NA_PATTERNS, 2 * GRID_W, NA_PAIR_KEYS), lambda h, g: (h, 0, 0, 0)),
        ],
        out_specs=pl.BlockSpec((TILE, HEAD_DIM), lambda h, g: (g, h)),
        out_shape=jax.ShapeDtypeStruct((t, C_W), BF16),
        scratch_shapes=[pltpu.VMEM((TILE + 2 * NA_HALO, HEAD_DIM), BF16),
                        pltpu.VMEM((TILE + 2 * NA_HALO, HEAD_DIM), BF16)],
        compiler_params=_cparams(("parallel", "parallel")),
        name="na_attn",
    )(proj, proj, proj, proj, proj, proj, proj, table)


def _pack_w_in(w):
    sl = lambda off, n: w[:, off:off + n]
    cols = [sl(IN_A_Q, A_Q), sl(IN_B_V, B_V), sl(IN_B_OG, B_V), sl(IN_C_Q, C_W), sl(IN_C_K, C_W), sl(IN_C_V, C_W),
            sl(IN_GL, 3 * D_MODEL), sl(IN_B_Q, B_QK), sl(IN_B_K, B_QK), sl(IN_A_K, A_KV), sl(IN_A_V, A_KV)]
    wp = jnp.concatenate(cols, axis=1).astype(BF16)
    wgr = jnp.pad(sl(IN_B_GR, 2 * B_GATE_RANK), ((0, 0), (0, GR_PAD - 2 * B_GATE_RANK))).astype(BF16)
    return wp, wgr


def _pack_w_ffn_in(w, bn):
    d = w.shape[0]
    g = w[:, :D_FF].reshape(d, D_FF // bn, 1, bn)
    u = w[:, D_FF:].reshape(d, D_FF // bn, 1, bn)
    return jnp.concatenate([g, u], axis=2).reshape(d, 2 * D_FF).astype(BF16)


def _pack_gla_w2(w2):
    z = jnp.zeros((GR_PAD, B_QK), F32)
    w2f = z.at[0:B_GATE_RANK].set(w2[0]).astype(BF16)
    w2b = z.at[B_GATE_RANK:2 * B_GATE_RANK].set(w2[1]).astype(BF16)
    return w2f, w2b


FFN_BN = 512


def kernel(x_prompt, x_sample, norm1, w_in, sink_a, gla_w2, gla_b, gla_norm, rpb_c, w_br_a, w_br_b, w_br_c, w_out,
           norm2, w_ffn_in, w_ffn_out, norm_f):
    bp, sp, d = x_prompt.shape
    bs, ss, _ = x_sample.shape
    tp, ts = bp * sp, bs * ss
    assert sp % TILE == 0 and ss % TILE == 0 and d == D_MODEL
    seqs = (tp, sp, ss)
    x = jnp.concatenate([x_prompt.reshape(tp, d), x_sample.reshape(ts, d)], axis=0)

    for l in range(DEPTH):
        wp, wgr = _pack_w_in(w_in[l])
        w2f, w2b = _pack_gla_w2(gla_w2[l])
        table = _na_bias_table(rpb_c[l])
        proj, gr = _inproj(x, norm1[l].reshape(1, d), wp, wgr)
        o_a = _win_attn(proj, sink_a[l], seqs)
        o_f, o_bw = _gla(proj, gr, w2f, w2b, gla_b[l], seqs)
        o_c = _na_attn(proj, table, seqs)
        merged = _merge(o_a, o_c, o_f, o_bw, proj, gla_norm[l].reshape(1, B_DV),
                        w_br_a[l].astype(BF16), w_br_b[l].astype(BF16), w_br_c[l].astype(BF16))
        x = _mm_res(merged, w_out[l].astype(BF16), x, 1024, 1024, "out_proj")
        act = _ffn_in(x, norm2[l].reshape(1, d), _pack_w_ffn_in(w_ffn_in[l], FFN_BN), bn=FFN_BN)
        x = _mm_res(act, w_ffn_out[l].astype(BF16), x, 1024, 512, "ffn_out")

    g = norm_f.reshape(1, d)
    y_p = _final_norm(x, g, 0, tp).reshape(bp, sp, d)
    y_s = _final_norm(x, g, tp, ts).reshape(bs, ss, d)
    return (y_p, y_s)
```

```python
import functools

import numpy as np
import jax
import jax.numpy as jnp
from jax import lax
from jax.experimental import pallas as pl
from jax.experimental.pallas import tpu as pltpu

F32 = jnp.float32
BF16 = jnp.bfloat16

D_MODEL = 2048
DEPTH = 2
GRID_W = 64
HEAD_DIM = 128
A_HEADS = 8
A_KV_HEADS = 2
A_GROUP = A_HEADS // A_KV_HEADS
A_WINDOW = 128
A_BLOCK = 128
B_HEADS = 4
B_DK = 128
B_DV = 256
B_CHUNK = 64
B_GATE_RANK = 16
B_GATE_TAU = 16.0
C_HEADS = 8
C_WIN_R = 8
C_WIN_C = 16
D_FF = -(-8 * D_MODEL // (3 * 256)) * 256
RMS_EPS = 1e-6
NEG_INF = -1e30
LOG2E = 1.4426950408889634

A_Q = A_HEADS * HEAD_DIM
A_KV = A_KV_HEADS * HEAD_DIM
B_QK = B_HEADS * B_DK
B_V = B_HEADS * B_DV
C_W = C_HEADS * HEAD_DIM
SPLIT_SIZES = (A_Q, A_KV, A_KV, B_QK, B_QK, B_V, B_V, 2 * B_GATE_RANK, C_W, C_W, C_W, 3 * D_MODEL)
SPLIT_OFF = tuple(int(i) for i in np.cumsum((0,) + SPLIT_SIZES))
(IN_A_Q, IN_A_K, IN_A_V, IN_B_Q, IN_B_K, IN_B_V, IN_B_OG, IN_B_GR, IN_C_Q, IN_C_K, IN_C_V, IN_GL) = SPLIT_OFF[:-1]

P_A_Q = 0
P_B_V = 1024
P_B_OG = 2048
P_C_Q = 3072
P_C_K = 4096
P_C_V = 5120
P_GATE = 6144
P_B_Q = 12288
P_B_K = 12800
P_A_K = 13312
P_A_V = 13568
P_COLS = 13824
GR_PAD = 128

TILE = 2048
NA_Q_ROWS = 4
NA_KEY_ROWS = 12
NA_Q = NA_Q_ROWS * GRID_W
NA_KEYS = NA_KEY_ROWS * GRID_W
NA_HALO = 256
NA_UNROLL = 2

VMEM_LIMIT = 56 * 1024 * 1024


def _cparams(sem):
    return pltpu.CompilerParams(dimension_semantics=sem, vmem_limit_bytes=VMEM_LIMIT)


def _seq_bounds(pos, seqs):
    tp, sp, ss = seqs
    in_p = pos < tp
    lo_p = lax.div(pos, sp) * sp
    lo_s = tp + lax.div(jnp.maximum(pos - tp, 0), ss) * ss
    lo = jnp.where(in_p, lo_p, lo_s)
    hi = lo + jnp.where(in_p, sp, ss)
    return lo, hi


NORM_ROWS = 256


def _rmsnorm_rows(x_ref, g_ref, h_ref):
    def body(c, carry):
        r = pl.multiple_of(c * NORM_ROWS, NORM_ROWS)
        xs = x_ref[pl.ds(r, NORM_ROWS), :]
        ms = jnp.mean(xs * xs, axis=-1, keepdims=True)
        h_ref[pl.ds(r, NORM_ROWS), :] = (xs * lax.rsqrt(ms + RMS_EPS) * g_ref[...]).astype(h_ref.dtype)
        return carry
    lax.fori_loop(0, x_ref.shape[0] // NORM_ROWS, body, 0)


def _inproj_kernel(x_ref, g_ref, w_ref, wgr_ref, proj_ref, gr_ref, h_ref):
    @pl.when(pl.program_id(1) == 0)
    def _():
        _rmsnorm_rows(x_ref, g_ref, h_ref)
        gr_ref[...] = jnp.dot(h_ref[...], wgr_ref[...], preferred_element_type=F32)
    proj_ref[...] = jnp.dot(h_ref[...], w_ref[...], preferred_element_type=F32).astype(proj_ref.dtype)


def _inproj(x, g, w, wgr, bm=1024, bn=1536):
    t, d = x.shape
    n = w.shape[1]
    return pl.pallas_call(
        _inproj_kernel,
        grid=(t // bm, n // bn),
        in_specs=[
            pl.BlockSpec((bm, d), lambda i, j: (i, 0)),
            pl.BlockSpec((1, d), lambda i, j: (0, 0)),
            pl.BlockSpec((d, bn), lambda i, j: (0, j)),
            pl.BlockSpec((d, GR_PAD), lambda i, j: (0, 0)),
        ],
        out_specs=[
            pl.BlockSpec((bm, bn), lambda i, j: (i, j)),
            pl.BlockSpec((bm, GR_PAD), lambda i, j: (i, 0)),
        ],
        out_shape=[jax.ShapeDtypeStruct((t, n), BF16), jax.ShapeDtypeStruct((t, GR_PAD), F32)],
        scratch_shapes=[pltpu.VMEM((bm, d), BF16)],
        compiler_params=_cparams(("parallel", "arbitrary")),
        name="inproj",
    )(x, g, w, wgr)


def _ffn_in_kernel(x_ref, g_ref, wg_ref, wu_ref, act_ref, h_ref):
    @pl.when(pl.program_id(1) == 0)
    def _():
        _rmsnorm_rows(x_ref, g_ref, h_ref)
    h = h_ref[...]
    gate = jnp.dot(h, wg_ref[...], preferred_element_type=F32)
    up = jnp.dot(h, wu_ref[...], preferred_element_type=F32)
    act_ref[...] = (gate * jax.nn.sigmoid(gate) * up).astype(act_ref.dtype)


def _ffn_in(x, g, w, bm=1024, bn=512):
    t, d = x.shape
    ff = w.shape[1] // 2
    nj = ff // bn
    return pl.pallas_call(
        _ffn_in_kernel,
        grid=(t // bm, nj),
        in_specs=[
            pl.BlockSpec((bm, d), lambda i, j: (i, 0)),
            pl.BlockSpec((1, d), lambda i, j: (0, 0)),
            pl.BlockSpec((d, bn), lambda i, j: (0, j)),
            pl.BlockSpec((d, bn), lambda i, j: (0, nj + j)),
        ],
        out_specs=pl.BlockSpec((bm, bn), lambda i, j: (i, j)),
        out_shape=jax.ShapeDtypeStruct((t, ff), BF16),
        scratch_shapes=[pltpu.VMEM((bm, d), BF16)],
        compiler_params=_cparams(("parallel", "arbitrary")),
        name="ffn_in",
    )(x, g, w, w)


def _mm_res_kernel(a_ref, w_ref, x_ref, o_ref):
    o_ref[...] = x_ref[...] + jnp.dot(a_ref[...], w_ref[...], preferred_element_type=F32)


def _mm_res(a, w, x, bm, bn, name):
    t, k = a.shape
    n = w.shape[1]
    return pl.pallas_call(
        _mm_res_kernel,
        grid=(t // bm, n // bn),
        in_specs=[
            pl.BlockSpec((bm, k), lambda i, j: (i, 0)),
            pl.BlockSpec((k, bn), lambda i, j: (0, j)),
            pl.BlockSpec((bm, bn), lambda i, j: (i, j)),
        ],
        out_specs=pl.BlockSpec((bm, bn), lambda i, j: (i, j)),
        out_shape=jax.ShapeDtypeStruct((t, n), F32),
        compiler_params=_cparams(("parallel", "arbitrary")),
        name=name,
    )(a, w, x)


MERGE_ROWS = 256


def _merge_kernel(oa_ref, oc_ref, of_ref, obw_ref, og_ref, gain_ref, ga_ref, gb_ref, gc_ref,
                  wa_ref, wb_ref, wc_ref, out_ref, ob_ref):
    @pl.when(pl.program_id(1) == 0)
    def _():
        def body(c, carry):
            r = pl.multiple_of(c * MERGE_ROWS, MERGE_ROWS)
            o = of_ref[pl.ds(r, MERGE_ROWS), :].astype(F32) + obw_ref[pl.ds(r, MERGE_ROWS), :].astype(F32)
            og = og_ref[pl.ds(r, MERGE_ROWS), :].astype(F32)
            for h in range(B_HEADS):
                sl = slice(h * B_DV, (h + 1) * B_DV)
                oh = o[:, sl]
                ms = jnp.mean(oh * oh, axis=-1, keepdims=True)
                ogh = og[:, sl]
                y = oh * lax.rsqrt(ms + RMS_EPS) * gain_ref[...] * (ogh * jax.nn.sigmoid(ogh))
                ob_ref[pl.ds(r, MERGE_ROWS), sl] = y.astype(ob_ref.dtype)
            return carry
        lax.fori_loop(0, of_ref.shape[0] // MERGE_ROWS, body, 0)

    ya = jnp.dot(oa_ref[...], wa_ref[...], preferred_element_type=F32)
    acc = jax.nn.sigmoid(ga_ref[...].astype(F32)) * ya
    yb = jnp.dot(ob_ref[...], wb_ref[...], preferred_element_type=F32)
    acc = acc + jax.nn.sigmoid(gb_ref[...].astype(F32)) * yb
    yc = jnp.dot(oc_ref[...], wc_ref[...], preferred_element_type=F32)
    acc = acc + jax.nn.sigmoid(gc_ref[...].astype(F32)) * yc
    out_ref[...] = acc.astype(out_ref.dtype)


def _merge(o_a, o_c, o_f, o_bw, proj, gain, wa, wb, wc, bm=1024, bn=512):
    t = o_a.shape[0]
    d = wa.shape[1]
    kb = o_a.shape[1]
    gate0 = P_GATE // bn
    gstep = D_MODEL // bn
    wide = lambda i, j: (i, 0)
    return pl.pallas_call(
        _merge_kernel,
        grid=(t // bm, d // bn),
        in_specs=[
            pl.BlockSpec((bm, kb), wide),
            pl.BlockSpec((bm, kb), wide),
            pl.BlockSpec((bm, kb), wide),
            pl.BlockSpec((bm, kb), wide),
            pl.BlockSpec((bm, B_V), lambda i, j: (i, P_B_OG // B_V)),
            pl.BlockSpec((1, B_DV), lambda i, j: (0, 0)),
            pl.BlockSpec((bm, bn), lambda i, j: (i, gate0 + j)),
            pl.BlockSpec((bm, bn), lambda i, j: (i, gate0 + gstep + j)),
            pl.BlockSpec((bm, bn), lambda i, j: (i, gate0 + 2 * gstep + j)),
            pl.BlockSpec((kb, bn), lambda i, j: (0, j)),
            pl.BlockSpec((kb, bn), lambda i, j: (0, j)),
            pl.BlockSpec((kb, bn), lambda i, j: (0, j)),
        ],
        out_specs=pl.BlockSpec((bm, bn), lambda i, j: (i, j)),
        out_shape=jax.ShapeDtypeStruct((t, d), BF16),
        scratch_shapes=[pltpu.VMEM((bm, kb), BF16)],
        compiler_params=_cparams(("parallel", "arbitrary")),
        name="merge",
    )(o_a, o_c, o_f, o_bw, proj, gain, proj, proj, proj, wa, wb, wc)


def _final_norm_kernel(x_ref, g_ref, y_ref):
    _rmsnorm_rows(x_ref, g_ref, y_ref)


def _final_norm(x, g, row0, rows, bm=512):
    d = x.shape[1]
    blk0 = row0 // bm
    return pl.pallas_call(
        _final_norm_kernel,
        grid=(rows // bm,),
        in_specs=[
            pl.BlockSpec((bm, d), lambda i: (blk0 + i, 0)),
            pl.BlockSpec((1, d), lambda i: (0, 0)),
        ],
        out_specs=pl.BlockSpec((bm, d), lambda i: (i, 0)),
        out_shape=jax.ShapeDtypeStruct((rows, d), F32),
        compiler_params=_cparams(("parallel",)),
        name="final_norm",
    )(x, g)


A_KEYS = 3 * A_BLOCK
A_ROWS = A_GROUP * A_BLOCK


def _nt_dot(a, b):
    return lax.dot_general(a, b, (((1,), (1,)), ((), ())), preferred_element_type=F32)


def _win_kernel(seqs, sink_ref, q_ref, kp_ref, kc_ref, kn_ref, vp_ref, vc_ref, vn_ref, o_ref, kbuf, vbuf):
    tile_start = pl.program_id(0) * TILE
    seq_lo, seq_hi = _seq_bounds(tile_start, seqs)
    kbuf[0:A_BLOCK, :] = kp_ref[...]
    kbuf[A_BLOCK:A_BLOCK + TILE, :] = kc_ref[...]
    kbuf[A_BLOCK + TILE:, :] = kn_ref[...]
    vbuf[0:A_BLOCK, :] = vp_ref[...]
    vbuf[A_BLOCK:A_BLOCK + TILE, :] = vc_ref[...]
    vbuf[A_BLOCK + TILE:, :] = vn_ref[...]

    row = lax.broadcasted_iota(jnp.int32, (A_ROWS, A_KEYS), 0)
    col = lax.broadcasted_iota(jnp.int32, (A_ROWS, A_KEYS), 1)
    dist = jnp.abs((row % A_BLOCK) - (col - A_BLOCK))
    in_window = dist <= A_WINDOW
    distf = dist.astype(F32)
    grp = lax.broadcasted_iota(jnp.int32, (A_ROWS, 1), 0) // A_BLOCK
    kcol = lax.broadcasted_iota(jnp.int32, (1, A_KEYS), 1)
    qk_scale = HEAD_DIM ** -0.5 * LOG2E

    nbias, sink = [], []
    for kvh in range(A_KV_HEADS):
        slope = jnp.zeros((A_ROWS, 1), F32)
        snk = jnp.zeros((A_ROWS, 1), F32)
        for g in range(A_GROUP):
            h = kvh * A_GROUP + g
            slope = jnp.where(grp == g, 2.0 ** (-8.0 * (h + 1) / A_HEADS) * LOG2E, slope)
            snk = jnp.where(grp == g, sink_ref[h] * LOG2E, snk)
        nbias.append(jnp.where(in_window, -slope * distf, NEG_INF))
        sink.append(snk)

    def body(n, carry):
        r = pl.multiple_of(n * A_BLOCK, A_BLOCK)
        kabs = tile_start - A_BLOCK + r + kcol
        kbias = jnp.where((kabs >= seq_lo) & (kabs < seq_hi), 0.0, NEG_INF)
        t = []
        for kvh in range(A_KV_HEADS):
            q4 = jnp.concatenate(
                [q_ref[pl.ds(r, A_BLOCK), (kvh * A_GROUP + g) * HEAD_DIM:(kvh * A_GROUP + g + 1) * HEAD_DIM]
                 for g in range(A_GROUP)], axis=0)
            kw = kbuf[pl.ds(r, A_KEYS), kvh * HEAD_DIM:(kvh + 1) * HEAD_DIM]
            t.append(_nt_dot(q4, kw) * qk_scale + nbias[kvh] + kbias)
        p, den = [], []
        for kvh in range(A_KV_HEADS):
            m = jnp.maximum(jnp.max(t[kvh], axis=-1, keepdims=True), sink[kvh])
            e = jnp.exp2(t[kvh] - m)
            den.append(jnp.sum(e, axis=-1, keepdims=True) + jnp.exp2(sink[kvh] - m))
            p.append(e.astype(BF16))
        for kvh in range(A_KV_HEADS):
            vw = vbuf[pl.ds(r, A_KEYS), kvh * HEAD_DIM:(kvh + 1) * HEAD_DIM]
            o = jnp.dot(p[kvh], vw, preferred_element_type=F32) / den[kvh]
            for g in range(A_GROUP):
                h = kvh * A_GROUP + g
                o_ref[pl.ds(r, A_BLOCK), h * HEAD_DIM:(h + 1) * HEAD_DIM] = (
                    o[g * A_BLOCK:(g + 1) * A_BLOCK].astype(o_ref.dtype))
        return carry
    lax.fori_loop(0, TILE // A_BLOCK, body, 0)


def _win_attn(proj, sink, seqs):
    t = proj.shape[0]
    nt = t // TILE
    per = TILE // A_BLOCK
    last = t // A_BLOCK - 1
    kblk = P_A_K // A_KV
    vblk = P_A_V // A_KV
    prev = lambda col: (lambda g: (jnp.maximum(g * per - 1, 0), col))
    cur = lambda col: (lambda g: (g, col))
    nxt = lambda col: (lambda g: (jnp.minimum((g + 1) * per, last), col))
    return pl.pallas_call(
        functools.partial(_win_kernel, seqs),
        grid=(nt,),
        in_specs=[
            pl.BlockSpec(memory_space=pltpu.SMEM),
            pl.BlockSpec((TILE, A_Q), lambda g: (g, P_A_Q // A_Q)),
            pl.BlockSpec((A_BLOCK, A_KV), prev(kblk)),
            pl.BlockSpec((TILE, A_KV), cur(kblk)),
            pl.BlockSpec((A_BLOCK, A_KV), nxt(kblk)),
            pl.BlockSpec((A_BLOCK, A_KV), prev(vblk)),
            pl.BlockSpec((TILE, A_KV), cur(vblk)),
            pl.BlockSpec((A_BLOCK, A_KV), nxt(vblk)),
        ],
        out_specs=pl.BlockSpec((TILE, A_Q), lambda g: (g, 0)),
        out_shape=jax.ShapeDtypeStruct((t, A_Q), BF16),
        scratch_shapes=[pltpu.VMEM((TILE + 2 * A_BLOCK, A_KV), BF16),
                        pltpu.VMEM((TILE + 2 * A_BLOCK, A_KV), BF16)],
        compiler_params=_cparams(("parallel",)),
        name="win_attn",
    )(sink, proj, proj, proj, proj, proj, proj, proj)


GLA_TILE = 1024
GLA_BLK = 256


def _tn_dot(a, b):
    return lax.dot_general(a, b, (((0,), (0,)), ((), ())), preferred_element_type=F32)


def _gla_prepare(q_ref, k_ref, v_ref, gr_ref, w2_ref, bias, tri, qt_ref, kd_ref, dec_ref, oi_ref):
    row = lax.broadcasted_iota(jnp.int32, (GLA_BLK, GLA_BLK), 0)
    col = lax.broadcasted_iota(jnp.int32, (GLA_BLK, GLA_BLK), 1)
    tri_bf = tri.astype(BF16)
    ones_bf = ((row // B_CHUNK) == (col // B_CHUNK)).astype(BF16)

    def body(i, carry):
        r = pl.multiple_of(i * GLA_BLK, GLA_BLK)
        rows = pl.ds(r, GLA_BLK)
        gr = gr_ref[rows, :].astype(BF16)
        logits = jnp.dot(gr, w2_ref[...], preferred_element_type=F32) + bias
        lg = (jnp.minimum(logits, 0.0) - jnp.log(1.0 + jnp.exp(-jnp.abs(logits)))) * (1.0 / B_GATE_TAU)
        hi = lg.astype(BF16)
        lo = (lg - hi.astype(F32)).astype(BF16)
        b = jnp.dot(tri_bf, hi, preferred_element_type=F32) + jnp.dot(tri_bf, lo, preferred_element_type=F32)
        tot = jnp.dot(ones_bf, hi, preferred_element_type=F32) + jnp.dot(ones_bf, lo, preferred_element_type=F32)
        q = q_ref[rows, :].astype(F32) * (B_DK ** -0.5)
        k = k_ref[rows, :].astype(F32)
        qt = (q * jnp.exp(b)).astype(BF16)
        kt = (k * jnp.exp(-b)).astype(BF16)
        qt_ref[rows, :] = qt
        kd_ref[rows, :] = (k * jnp.exp(tot - b)).astype(BF16)
        dec_ref[rows, :] = jnp.exp(tot)
        for h in range(B_HEADS):
            ks = slice(h * B_DK, (h + 1) * B_DK)
            vs = slice(h * B_DV, (h + 1) * B_DV)
            a = jnp.where(tri, _nt_dot(qt[:, ks], kt[:, ks]), 0.0).astype(BF16)
            oi_ref[rows, vs] = jnp.dot(a, v_ref[rows, vs], preferred_element_type=F32)
        return carry
    lax.fori_loop(0, GLA_TILE // GLA_BLK, body, 0)


def _gla_scan_chunk(r, v_ref, qt_ref, kd_ref, dec_ref, oi_ref, o_ref, st_ref):
    rows = pl.ds(r, B_CHUNK)
    dec = dec_ref[pl.ds(r, 1), :]
    for h in range(B_HEADS):
        ks = slice(h * B_DK, (h + 1) * B_DK)
        vs = slice(h * B_DV, (h + 1) * B_DV)
        st = st_ref[h]
        o = oi_ref[rows, vs] + _nt_dot(qt_ref[rows, ks], st.astype(BF16))
        o_ref[rows, vs] = o.astype(o_ref.dtype)
        st_ref[h] = st * dec[:, ks] + _tn_dot(v_ref[rows, vs], kd_ref[rows, ks])


def _gla_kernel(seqs, qf_ref, kf_ref, vf_ref, grf_ref, qb_ref, kb_ref, vb_ref, grb_ref, w2f_ref, w2b_ref,
                bias_ref, of_ref, ob_ref, sf_ref, sb_ref, qtf, kdf, decf, oif, qtb, kdb, decb, oib):
    i = pl.program_id(0)
    nt = pl.num_programs(0)
    f_start = i * GLA_TILE
    b_start = (nt - 1 - i) * GLA_TILE
    f_lo, _ = _seq_bounds(f_start, seqs)
    _, b_hi = _seq_bounds(b_start, seqs)

    @pl.when(f_start == f_lo)
    def _():
        sf_ref[...] = jnp.zeros_like(sf_ref)

    @pl.when(b_start + GLA_TILE == b_hi)
    def _():
        sb_ref[...] = jnp.zeros_like(sb_ref)

    row = lax.broadcasted_iota(jnp.int32, (GLA_BLK, GLA_BLK), 0)
    col = lax.broadcasted_iota(jnp.int32, (GLA_BLK, GLA_BLK), 1)
    same = (row // B_CHUNK) == (col // B_CHUNK)
    _gla_prepare(qf_ref, kf_ref, vf_ref, grf_ref, w2f_ref, bias_ref[0:1, :], same & (row >= col), qtf, kdf, decf, oif)
    _gla_prepare(qb_ref, kb_ref, vb_ref, grb_ref, w2b_ref, bias_ref[1:2, :], same & (row <= col), qtb, kdb, decb, oib)

    nchunk = GLA_TILE // B_CHUNK

    def body(c, carry):
        rf = pl.multiple_of(c * B_CHUNK, B_CHUNK)
        _gla_scan_chunk(rf, vf_ref, qtf, kdf, decf, oif, of_ref, sf_ref)
        rb = pl.multiple_of((nchunk - 1 - c) * B_CHUNK, B_CHUNK)
        _gla_scan_chunk(rb, vb_ref, qtb, kdb, decb, oib, ob_ref, sb_ref)
        return carry
    lax.fori_loop(0, nchunk, body, 0, unroll=2)


def _gla(proj, gr, w2f, w2b, bias, seqs):
    t = proj.shape[0]
    nt = t // GLA_TILE
    fwd = lambda col: (lambda i: (i, col))
    bwd = lambda col: (lambda i: (nt - 1 - i, col))
    qblk, kblk, vblk = P_B_Q // B_QK, P_B_K // B_QK, P_B_V // B_V

    def specs(mk):
        return [pl.BlockSpec((GLA_TILE, B_QK), mk(qblk)), pl.BlockSpec((GLA_TILE, B_QK), mk(kblk)),
                pl.BlockSpec((GLA_TILE, B_V), mk(vblk)), pl.BlockSpec((GLA_TILE, GR_PAD), mk(0))]

    const = lambda i: (0, 0)
    state = pltpu.VMEM((B_HEADS, B_DV, B_DK), F32)
    per_dir = [pltpu.VMEM((GLA_TILE, B_QK), BF16), pltpu.VMEM((GLA_TILE, B_QK), BF16),
               pltpu.VMEM((GLA_TILE, B_QK), F32), pltpu.VMEM((GLA_TILE, B_V), F32)]
    return pl.pallas_call(
        functools.partial(_gla_kernel, seqs),
        grid=(nt,),
        in_specs=specs(fwd) + specs(bwd) + [
            pl.BlockSpec((GR_PAD, B_QK), const), pl.BlockSpec((GR_PAD, B_QK), const),
            pl.BlockSpec((2, B_QK), const)],
        out_specs=[pl.BlockSpec((GLA_TILE, B_V), fwd(0)), pl.BlockSpec((GLA_TILE, B_V), bwd(0))],
        out_shape=[jax.ShapeDtypeStruct((t, B_V), BF16), jax.ShapeDtypeStruct((t, B_V), BF16)],
        scratch_shapes=[state, state] + per_dir + per_dir,
        compiler_params=_cparams(("arbitrary",)),
        name="gla",
    )(proj, proj, proj, gr, proj, proj, proj, gr, w2f, w2b, bias)


NA_PATTERNS = 3
_NA_ROWS_FOR_TABLE = 32
_NA_PATTERN_ROWS = (0, 2 * NA_Q_ROWS, _NA_ROWS_FOR_TABLE - NA_Q_ROWS)


def _na_bias_table(rpb):
    R = _NA_ROWS_FOR_TABLE
    kr = C_WIN_R
    key_rows = NA_KEY_ROWS
    rpb = rpb.astype(F32) * LOG2E
    n_dr = rpb.shape[1]
    edge = GRID_W - C_WIN_C
    ext = jnp.concatenate([jnp.repeat(rpb[:, :, :1], edge, axis=2), rpb, jnp.repeat(rpb[:, :, -1:], edge, axis=2)], axis=2)
    col = jnp.stack([ext[:, :, GRID_W - 1 - cq:2 * GRID_W - 1 - cq] for cq in range(GRID_W)], axis=2)
    cq = np.arange(GRID_W)[:, None]
    ck = np.arange(GRID_W)[None, :]
    col_start = np.clip(cq - C_WIN_C // 2, 0, GRID_W - C_WIN_C)
    col_ok = (ck >= col_start) & (ck < col_start + C_WIN_C)
    col = jnp.where(col_ok[None, None], col, NEG_INF)
    col = jnp.concatenate([col, jnp.full_like(col[:, :1], NEG_INF)], axis=1)
    slabs = []
    for r in _NA_PATTERN_ROWS:
        start = int(np.clip(r - kr // 2, 0, R - key_rows))
        for jj in range(NA_Q_ROWS):
            qrow = r + jj
            ws = int(np.clip(qrow - kr // 2, 0, R - kr))
            for ki in range(key_rows):
                krow = start + ki
                slabs.append(krow - qrow + (C_WIN_R - 1) if ws <= krow < ws + kr else n_dr)
    tab = jnp.stack([col[:, a] for a in slabs], axis=1)
    tab = tab.reshape(C_HEADS, NA_PATTERNS, NA_Q_ROWS, key_rows, GRID_W, GRID_W).transpose(0, 1, 2, 4, 3, 5)
    return tab.reshape(C_HEADS, NA_PATTERNS, NA_Q, NA_KEYS)


def _na_kernel(seqs, q_ref, kp_ref, kc_ref, kn_ref, vp_ref, vc_ref, vn_ref, tab_ref, o_ref, kbuf, vbuf):
    tile_start = pl.program_id(1) * TILE
    seq_lo, seq_hi = _seq_bounds(tile_start, seqs)
    r0 = lax.div(tile_start, GRID_W)
    row_lo = lax.div(seq_lo, GRID_W)
    row_hi = lax.div(seq_hi, GRID_W)
    kbuf[0:NA_HALO, :] = kp_ref[...]
    kbuf[NA_HALO:NA_HALO + TILE, :] = kc_ref[...]
    kbuf[NA_HALO + TILE:, :] = kn_ref[...]
    vbuf[0:NA_HALO, :] = vp_ref[...]
    vbuf[NA_HALO:NA_HALO + TILE, :] = vc_ref[...]
    vbuf[NA_HALO + TILE:, :] = vn_ref[...]
    qk_scale = HEAD_DIM ** -0.5 * LOG2E

    def body(it, carry):
        qs, off, pid = [], [], []
        for u in range(NA_UNROLL):
            grp = it * NA_UNROLL + u
            r = r0 + NA_Q_ROWS * grp
            start = jnp.clip(r - C_WIN_R // 2, row_lo, row_hi - NA_KEY_ROWS)
            off.append(pl.multiple_of((start - r0 + NA_HALO // GRID_W) * GRID_W, NA_Q))
            pid.append(jnp.where(r == row_lo, 0, jnp.where(r == row_hi - NA_Q_ROWS, 2, 1)))
            qs.append(pl.multiple_of(grp * NA_Q, NA_Q))
        t = [_nt_dot(q_ref[pl.ds(qs[u], NA_Q), :], kbuf[pl.ds(off[u], NA_KEYS), :]) * qk_scale + tab_ref[0, pid[u]]
             for u in range(NA_UNROLL)]
        p, den = [], []
        for u in range(NA_UNROLL):
            e = jnp.exp2(t[u] - jnp.max(t[u], axis=-1, keepdims=True))
            den.append(jnp.sum(e, axis=-1, keepdims=True))
            p.append(e.astype(BF16))
        for u in range(NA_UNROLL):
            o = jnp.dot(p[u], vbuf[pl.ds(off[u], NA_KEYS), :], preferred_element_type=F32) / den[u]
            o_ref[pl.ds(qs[u], NA_Q), :] = o.astype(o_ref.dtype)
        return carry
    lax.fori_loop(0, TILE // (NA_Q * NA_UNROLL), body, 0)


def _na_attn(proj, table, seqs):
    t = proj.shape[0]
    nt = t // TILE
    per = TILE // NA_HALO
    last = t // NA_HALO - 1
    qblk, kblk, vblk = P_C_Q // HEAD_DIM, P_C_K // HEAD_DIM, P_C_V // HEAD_DIM
    prev = lambda c0: (lambda h, g: (jnp.maximum(g * per - 1, 0), c0 + h))
    cur = lambda c0: (lambda h, g: (g, c0 + h))
    nxt = lambda c0: (lambda h, g: (jnp.minimum((g + 1) * per, last), c0 + h))
    return pl.pallas_call(
        functools.partial(_na_kernel, seqs),
        grid=(C_HEADS, nt),
        in_specs=[
            pl.BlockSpec((TILE, HEAD_DIM), cur(qblk)),
            pl.BlockSpec((NA_HALO, HEAD_DIM), prev(kblk)),
            pl.BlockSpec((TILE, HEAD_DIM), cur(kblk)),
            pl.BlockSpec((NA_HALO, HEAD_DIM), nxt(kblk)),
            pl.BlockSpec((NA_HALO, HEAD_DIM), prev(vblk)),
            pl.BlockSpec((TILE, HEAD_DIM), cur(vblk)),
            pl.BlockSpec((NA_HALO, HEAD_DIM), nxt(vblk)),
            pl.BlockSpec((1, NA_PATTERNS, NA_Q, NA_KEYS), lambda h, g: (h, 0, 0, 0)),
        ],
        out_specs=pl.BlockSpec((TILE, HEAD_DIM), lambda h, g: (g, h)),
        out_shape=jax.ShapeDtypeStruct((t, C_W), BF16),
        scratch_shapes=[pltpu.VMEM((TILE + 2 * NA_HALO, HEAD_DIM), BF16),
                        pltpu.VMEM((TILE + 2 * NA_HALO, HEAD_DIM), BF16)],
        compiler_params=_cparams(("parallel", "parallel")),
        name="na_attn",
    )(proj, proj, proj, proj, proj, proj, proj, table)


def _pack_w_in(w):
    sl = lambda off, n: w[:, off:off + n]
    cols = [sl(IN_A_Q, A_Q), sl(IN_B_V, B_V), sl(IN_B_OG, B_V), sl(IN_C_Q, C_W), sl(IN_C_K, C_W), sl(IN_C_V, C_W),
            sl(IN_GL, 3 * D_MODEL), sl(IN_B_Q, B_QK), sl(IN_B_K, B_QK), sl(IN_A_K, A_KV), sl(IN_A_V, A_KV)]
    wp = jnp.concatenate(cols, axis=1).astype(BF16)
    wgr = jnp.pad(sl(IN_B_GR, 2 * B_GATE_RANK), ((0, 0), (0, GR_PAD - 2 * B_GATE_RANK))).astype(BF16)
    return wp, wgr


def _pack_gla_w2(w2):
    z = jnp.zeros((GR_PAD, B_QK), F32)
    w2f = z.at[0:B_GATE_RANK].set(w2[0]).astype(BF16)
    w2b = z.at[B_GATE_RANK:2 * B_GATE_RANK].set(w2[1]).astype(BF16)
    return w2f, w2b


def kernel(x_prompt, x_sample, norm1, w_in, sink_a, gla_w2, gla_b, gla_norm, rpb_c, w_br_a, w_br_b, w_br_c, w_out,
           norm2, w_ffn_in, w_ffn_out, norm_f):
    bp, sp, d = x_prompt.shape
    bs, ss, _ = x_sample.shape
    tp, ts = bp * sp, bs * ss
    assert sp % TILE == 0 and ss % TILE == 0 and d == D_MODEL
    seqs = (tp, sp, ss)
    x = jnp.concatenate([x_prompt.reshape(tp, d), x_sample.reshape(ts, d)], axis=0)

    for l in range(DEPTH):
        wp, wgr = _pack_w_in(w_in[l])
        w2f, w2b = _pack_gla_w2(gla_w2[l])
        table = _na_bias_table(rpb_c[l])
        proj, gr = _inproj(x, norm1[l].reshape(1, d), wp, wgr)
        o_a = _win_attn(proj, sink_a[l], seqs)
        o_f, o_bw = _gla(proj, gr, w2f, w2b, gla_b[l], seqs)
        o_c = _na_attn(proj, table, seqs)
        merged = _merge(o_a, o_c, o_f, o_bw, proj, gla_norm[l].reshape(1, B_DV),
                        w_br_a[l].astype(BF16), w_br_b[l].astype(BF16), w_br_c[l].astype(BF16))
        x = _mm_res(merged, w_out[l].astype(BF16), x, 1024, 1024, "out_proj")
        act = _ffn_in(x, norm2[l].reshape(1, d), w_ffn_in[l].astype(BF16))
        x = _mm_res(act, w_ffn_out[l].astype(BF16), x, 1024, 512, "ffn_out")

    g = norm_f.reshape(1, d)
    y_p = _final_norm(x, g, 0, tp).reshape(bp, sp, d)
    y_s = _final_norm(x, g, tp, ts).reshape(bs, ss, d)
    return (y_p, y_s)
```

```python
import functools

import numpy as np
import jax
import jax.numpy as jnp
from jax import lax
from jax.experimental import pallas as pl
from jax.experimental.pallas import tpu as pltpu

F32 = jnp.float32
BF16 = jnp.bfloat16

D_MODEL = 2048
DEPTH = 2
GRID_W = 64
HEAD_DIM = 128
A_HEADS = 8
A_KV_HEADS = 2
A_GROUP = A_HEADS // A_KV_HEADS
A_WINDOW = 128
A_BLOCK = 128
B_HEADS = 4
B_DK = 128
B_DV = 256
B_CHUNK = 64
B_GATE_RANK = 16
B_GATE_TAU = 16.0
C_HEADS = 8
C_WIN_R = 8
C_WIN_C = 16
D_FF = -(-8 * D_MODEL // (3 * 256)) * 256
RMS_EPS = 1e-6
NEG_INF = -1e30
LOG2E = 1.4426950408889634

A_Q = A_HEADS * HEAD_DIM
A_KV = A_KV_HEADS * HEAD_DIM
B_QK = B_HEADS * B_DK
B_V = B_HEADS * B_DV
C_W = C_HEADS * HEAD_DIM
SPLIT_SIZES = (A_Q, A_KV, A_KV, B_QK, B_QK, B_V, B_V, 2 * B_GATE_RANK, C_W, C_W, C_W, 3 * D_MODEL)
SPLIT_OFF = tuple(int(i) for i in np.cumsum((0,) + SPLIT_SIZES))
(IN_A_Q, IN_A_K, IN_A_V, IN_B_Q, IN_B_K, IN_B_V, IN_B_OG, IN_B_GR, IN_C_Q, IN_C_K, IN_C_V, IN_GL) = SPLIT_OFF[:-1]

P_A_Q = 0
P_B_V = 1024
P_B_OG = 2048
P_C_Q = 3072
P_C_K = 4096
P_C_V = 5120
P_GATE = 6144
P_B_Q = 12288
P_B_K = 12800
P_A_K = 13312
P_A_V = 13568
P_COLS = 13824
GR_PAD = 128

TILE = 2048
NA_Q_ROWS = 4
NA_KEY_ROWS = 12
NA_Q = NA_Q_ROWS * GRID_W
NA_KEYS = NA_KEY_ROWS * GRID_W
NA_HALO = 256
NA_UNROLL = 4

VMEM_LIMIT = 56 * 1024 * 1024


def _cparams(sem):
    return pltpu.CompilerParams(dimension_semantics=sem, vmem_limit_bytes=VMEM_LIMIT)


def _seq_bounds(pos, seqs):
    tp, sp, ss = seqs
    in_p = pos < tp
    lo_p = lax.div(pos, sp) * sp
    lo_s = tp + lax.div(jnp.maximum(pos - tp, 0), ss) * ss
    lo = jnp.where(in_p, lo_p, lo_s)
    hi = lo + jnp.where(in_p, sp, ss)
    return lo, hi


NORM_ROWS = 256


def _rmsnorm_rows(x_ref, g_ref, h_ref):
    def body(c, carry):
        r = pl.multiple_of(c * NORM_ROWS, NORM_ROWS)
        xs = x_ref[pl.ds(r, NORM_ROWS), :]
        ms = jnp.mean(xs * xs, axis=-1, keepdims=True)
        h_ref[pl.ds(r, NORM_ROWS), :] = (xs * lax.rsqrt(ms + RMS_EPS) * g_ref[...]).astype(h_ref.dtype)
        return carry
    lax.fori_loop(0, x_ref.shape[0] // NORM_ROWS, body, 0)


def _inproj_kernel(x_ref, g_ref, w_ref, wgr_ref, proj_ref, gr_ref, h_ref):
    @pl.when(pl.program_id(1) == 0)
    def _():
        _rmsnorm_rows(x_ref, g_ref, h_ref)
        gr_ref[...] = jnp.dot(h_ref[...], wgr_ref[...], preferred_element_type=F32)
    proj_ref[...] = jnp.dot(h_ref[...], w_ref[...], preferred_element_type=F32).astype(proj_ref.dtype)


def _inproj(x, g, w, wgr, bm=1024, bn=1536):
    t, d = x.shape
    n = w.shape[1]
    return pl.pallas_call(
        _inproj_kernel,
        grid=(t // bm, n // bn),
        in_specs=[
            pl.BlockSpec((bm, d), lambda i, j: (i, 0)),
            pl.BlockSpec((1, d), lambda i, j: (0, 0)),
            pl.BlockSpec((d, bn), lambda i, j: (0, j)),
            pl.BlockSpec((d, GR_PAD), lambda i, j: (0, 0)),
        ],
        out_specs=[
            pl.BlockSpec((bm, bn), lambda i, j: (i, j)),
            pl.BlockSpec((bm, GR_PAD), lambda i, j: (i, 0)),
        ],
        out_shape=[jax.ShapeDtypeStruct((t, n), BF16), jax.ShapeDtypeStruct((t, GR_PAD), F32)],
        scratch_shapes=[pltpu.VMEM((bm, d), BF16)],
        compiler_params=_cparams(("parallel", "arbitrary")),
        name="inproj",
    )(x, g, w, wgr)


def _ffn_in_kernel(x_ref, g_ref, wg_ref, wu_ref, act_ref, h_ref):
    @pl.when(pl.program_id(1) == 0)
    def _():
        _rmsnorm_rows(x_ref, g_ref, h_ref)
    h = h_ref[...]
    gate = jnp.dot(h, wg_ref[...], preferred_element_type=F32)
    up = jnp.dot(h, wu_ref[...], preferred_element_type=F32)
    act_ref[...] = (gate * jax.nn.sigmoid(gate) * up).astype(act_ref.dtype)


def _ffn_in(x, g, w, bm=1024, bn=512):
    t, d = x.shape
    ff = w.shape[1] // 2
    nj = ff // bn
    return pl.pallas_call(
        _ffn_in_kernel,
        grid=(t // bm, nj),
        in_specs=[
            pl.BlockSpec((bm, d), lambda i, j: (i, 0)),
            pl.BlockSpec((1, d), lambda i, j: (0, 0)),
            pl.BlockSpec((d, bn), lambda i, j: (0, j)),
            pl.BlockSpec((d, bn), lambda i, j: (0, nj + j)),
        ],
        out_specs=pl.BlockSpec((bm, bn), lambda i, j: (i, j)),
        out_shape=jax.ShapeDtypeStruct((t, ff), BF16),
        scratch_shapes=[pltpu.VMEM((bm, d), BF16)],
        compiler_params=_cparams(("parallel", "arbitrary")),
        name="ffn_in",
    )(x, g, w, w)


def _mm_res_kernel(a_ref, w_ref, x_ref, o_ref):
    o_ref[...] = x_ref[...] + jnp.dot(a_ref[...], w_ref[...], preferred_element_type=F32)


def _mm_res(a, w, x, bm, bn, name):
    t, k = a.shape
    n = w.shape[1]
    return pl.pallas_call(
        _mm_res_kernel,
        grid=(t // bm, n // bn),
        in_specs=[
            pl.BlockSpec((bm, k), lambda i, j: (i, 0)),
            pl.BlockSpec((k, bn), lambda i, j: (0, j)),
            pl.BlockSpec((bm, bn), lambda i, j: (i, j)),
        ],
        out_specs=pl.BlockSpec((bm, bn), lambda i, j: (i, j)),
        out_shape=jax.ShapeDtypeStruct((t, n), F32),
        compiler_params=_cparams(("parallel", "arbitrary")),
        name=name,
    )(a, w, x)


MERGE_ROWS = 256


def _merge_kernel(oa_ref, oc_ref, of_ref, obw_ref, og_ref, gain_ref, ga_ref, gb_ref, gc_ref,
                  wa_ref, wb_ref, wc_ref, out_ref, ob_ref):
    @pl.when(pl.program_id(1) == 0)
    def _():
        def body(c, carry):
            r = pl.multiple_of(c * MERGE_ROWS, MERGE_ROWS)
            o = of_ref[pl.ds(r, MERGE_ROWS), :].astype(F32) + obw_ref[pl.ds(r, MERGE_ROWS), :].astype(F32)
            og = og_ref[pl.ds(r, MERGE_ROWS), :].astype(F32)
            for h in range(B_HEADS):
                sl = slice(h * B_DV, (h + 1) * B_DV)
                oh = o[:, sl]
                ms = jnp.mean(oh * oh, axis=-1, keepdims=True)
                ogh = og[:, sl]
                y = oh * lax.rsqrt(ms + RMS_EPS) * gain_ref[...] * (ogh * jax.nn.sigmoid(ogh))
                ob_ref[pl.ds(r, MERGE_ROWS), sl] = y.astype(ob_ref.dtype)
            return carry
        lax.fori_loop(0, of_ref.shape[0] // MERGE_ROWS, body, 0)

    ya = jnp.dot(oa_ref[...], wa_ref[...], preferred_element_type=F32)
    acc = jax.nn.sigmoid(ga_ref[...].astype(F32)) * ya
    yb = jnp.dot(ob_ref[...], wb_ref[...], preferred_element_type=F32)
    acc = acc + jax.nn.sigmoid(gb_ref[...].astype(F32)) * yb
    yc = jnp.dot(oc_ref[...], wc_ref[...], preferred_element_type=F32)
    acc = acc + jax.nn.sigmoid(gc_ref[...].astype(F32)) * yc
    out_ref[...] = acc.astype(out_ref.dtype)


def _merge(o_a, o_c, o_f, o_bw, proj, gain, wa, wb, wc, bm=1024, bn=512):
    t = o_a.shape[0]
    d = wa.shape[1]
    kb = o_a.shape[1]
    gate0 = P_GATE // bn
    gstep = D_MODEL // bn
    wide = lambda i, j: (i, 0)
    return pl.pallas_call(
        _merge_kernel,
        grid=(t // bm, d // bn),
        in_specs=[
            pl.BlockSpec((bm, kb), wide),
            pl.BlockSpec((bm, kb), wide),
            pl.BlockSpec((bm, kb), wide),
            pl.BlockSpec((bm, kb), wide),
            pl.BlockSpec((bm, B_V), lambda i, j: (i, P_B_OG // B_V)),
            pl.BlockSpec((1, B_DV), lambda i, j: (0, 0)),
            pl.BlockSpec((bm, bn), lambda i, j: (i, gate0 + j)),
            pl.BlockSpec((bm, bn), lambda i, j: (i, gate0 + gstep + j)),
            pl.BlockSpec((bm, bn), lambda i, j: (i, gate0 + 2 * gstep + j)),
            pl.BlockSpec((kb, bn), lambda i, j: (0, j)),
            pl.BlockSpec((kb, bn), lambda i, j: (0, j)),
            pl.BlockSpec((kb, bn), lambda i, j: (0, j)),
        ],
        out_specs=pl.BlockSpec((bm, bn), lambda i, j: (i, j)),
        out_shape=jax.ShapeDtypeStruct((t, d), BF16),
        scratch_shapes=[pltpu.VMEM((bm, kb), BF16)],
        compiler_params=_cparams(("parallel", "arbitrary")),
        name="merge",
    )(o_a, o_c, o_f, o_bw, proj, gain, proj, proj, proj, wa, wb, wc)


def _final_norm_kernel(x_ref, g_ref, y_ref):
    _rmsnorm_rows(x_ref, g_ref, y_ref)


def _final_norm(x, g, row0, rows, bm=512):
    d = x.shape[1]
    blk0 = row0 // bm
    return pl.pallas_call(
        _final_norm_kernel,
        grid=(rows // bm,),
        in_specs=[
            pl.BlockSpec((bm, d), lambda i: (blk0 + i, 0)),
            pl.BlockSpec((1, d), lambda i: (0, 0)),
        ],
        out_specs=pl.BlockSpec((bm, d), lambda i: (i, 0)),
        out_shape=jax.ShapeDtypeStruct((rows, d), F32),
        compiler_params=_cparams(("parallel",)),
        name="final_norm",
    )(x, g)


A_KEYS = 3 * A_BLOCK
A_ROWS = A_GROUP * A_BLOCK


def _nt_dot(a, b):
    return lax.dot_general(a, b, (((1,), (1,)), ((), ())), preferred_element_type=F32)


def _win_kernel(seqs, sink_ref, q_ref, kp_ref, kc_ref, kn_ref, vp_ref, vc_ref, vn_ref, o_ref, kbuf, vbuf):
    tile_start = pl.program_id(0) * TILE
    seq_lo, seq_hi = _seq_bounds(tile_start, seqs)
    kbuf[0:A_BLOCK, :] = kp_ref[...]
    kbuf[A_BLOCK:A_BLOCK + TILE, :] = kc_ref[...]
    kbuf[A_BLOCK + TILE:, :] = kn_ref[...]
    vbuf[0:A_BLOCK, :] = vp_ref[...]
    vbuf[A_BLOCK:A_BLOCK + TILE, :] = vc_ref[...]
    vbuf[A_BLOCK + TILE:, :] = vn_ref[...]

    row = lax.broadcasted_iota(jnp.int32, (A_ROWS, A_KEYS), 0)
    col = lax.broadcasted_iota(jnp.int32, (A_ROWS, A_KEYS), 1)
    dist = jnp.abs((row % A_BLOCK) - (col - A_BLOCK))
    in_window = dist <= A_WINDOW
    distf = dist.astype(F32)
    grp = lax.broadcasted_iota(jnp.int32, (A_ROWS, 1), 0) // A_BLOCK
    kcol = lax.broadcasted_iota(jnp.int32, (1, A_KEYS), 1)
    qk_scale = HEAD_DIM ** -0.5 * LOG2E

    nbias, sink = [], []
    for kvh in range(A_KV_HEADS):
        slope = jnp.zeros((A_ROWS, 1), F32)
        snk = jnp.zeros((A_ROWS, 1), F32)
        for g in range(A_GROUP):
            h = kvh * A_GROUP + g
            slope = jnp.where(grp == g, 2.0 ** (-8.0 * (h + 1) / A_HEADS) * LOG2E, slope)
            snk = jnp.where(grp == g, sink_ref[h] * LOG2E, snk)
        nbias.append(jnp.where(in_window, -slope * distf, NEG_INF))
        sink.append(snk)

    def body(n, carry):
        r = pl.multiple_of(n * A_BLOCK, A_BLOCK)
        kabs = tile_start - A_BLOCK + r + kcol
        kbias = jnp.where((kabs >= seq_lo) & (kabs < seq_hi), 0.0, NEG_INF)
        t = []
        for kvh in range(A_KV_HEADS):
            q4 = jnp.concatenate(
                [q_ref[pl.ds(r, A_BLOCK), (kvh * A_GROUP + g) * HEAD_DIM:(kvh * A_GROUP + g + 1) * HEAD_DIM]
                 for g in range(A_GROUP)], axis=0)
            kw = kbuf[pl.ds(r, A_KEYS), kvh * HEAD_DIM:(kvh + 1) * HEAD_DIM]
            t.append(_nt_dot(q4, kw) * qk_scale + nbias[kvh] + kbias)
        p, den = [], []
        for kvh in range(A_KV_HEADS):
            m = jnp.maximum(jnp.max(t[kvh], axis=-1, keepdims=True), sink[kvh])
            e = jnp.exp2(t[kvh] - m)
            den.append(jnp.sum(e, axis=-1, keepdims=True) + jnp.exp2(sink[kvh] - m))
            p.append(e.astype(BF16))
        for kvh in range(A_KV_HEADS):
            vw = vbuf[pl.ds(r, A_KEYS), kvh * HEAD_DIM:(kvh + 1) * HEAD_DIM]
            o = jnp.dot(p[kvh], vw, preferred_element_type=F32) / den[kvh]
            for g in range(A_GROUP):
                h = kvh * A_GROUP + g
                o_ref[pl.ds(r, A_BLOCK), h * HEAD_DIM:(h + 1) * HEAD_DIM] = (
                    o[g * A_BLOCK:(g + 1) * A_BLOCK].astype(o_ref.dtype))
        return carry
    lax.fori_loop(0, TILE // A_BLOCK, body, 0)


def _win_attn(proj, sink, seqs):
    t = proj.shape[0]
    nt = t // TILE
    per = TILE // A_BLOCK
    last = t // A_BLOCK - 1
    kblk = P_A_K // A_KV
    vblk = P_A_V // A_KV
    prev = lambda col: (lambda g: (jnp.maximum(g * per - 1, 0), col))
    cur = lambda col: (lambda g: (g, col))
    nxt = lambda col: (lambda g: (jnp.minimum((g + 1) * per, last), col))
    return pl.pallas_call(
        functools.partial(_win_kernel, seqs),
        grid=(nt,),
        in_specs=[
            pl.BlockSpec(memory_space=pltpu.SMEM),
            pl.BlockSpec((TILE, A_Q), lambda g: (g, P_A_Q // A_Q)),
            pl.BlockSpec((A_BLOCK, A_KV), prev(kblk)),
            pl.BlockSpec((TILE, A_KV), cur(kblk)),
            pl.BlockSpec((A_BLOCK, A_KV), nxt(kblk)),
            pl.BlockSpec((A_BLOCK, A_KV), prev(vblk)),
            pl.BlockSpec((TILE, A_KV), cur(vblk)),
            pl.BlockSpec((A_BLOCK, A_KV), nxt(vblk)),
        ],
        out_specs=pl.BlockSpec((TILE, A_Q), lambda g: (g, 0)),
        out_shape=jax.ShapeDtypeStruct((t, A_Q), BF16),
        scratch_shapes=[pltpu.VMEM((TILE + 2 * A_BLOCK, A_KV), BF16),
                        pltpu.VMEM((TILE + 2 * A_BLOCK, A_KV), BF16)],
        compiler_params=_cparams(("parallel",)),
        name="win_attn",
    )(sink, proj, proj, proj, proj, proj, proj, proj)


GLA_TILE = 1024
GLA_BLK = 256


def _tn_dot(a, b):
    return lax.dot_general(a, b, (((0,), (0,)), ((), ())), preferred_element_type=F32)


def _gla_prepare(dirs, bias_ref):
    row = lax.broadcasted_iota(jnp.int32, (GLA_BLK, GLA_BLK), 0)
    col = lax.broadcasted_iota(jnp.int32, (GLA_BLK, GLA_BLK), 1)
    same = (row // B_CHUNK) == (col // B_CHUNK)
    tris = [same & ((row >= col) if d[-1] else (row <= col)) for d in dirs]
    tris_bf = [t.astype(BF16) for t in tris]
    biases = [bias_ref[n:n + 1, :] for n in range(len(dirs))]

    def body(i, carry):
        r = pl.multiple_of(i * GLA_BLK, GLA_BLK)
        rows = pl.ds(r, GLA_BLK)
        his, los = [], []
        for (q_ref, k_ref, v_ref, gr_ref, w2_ref, qt_ref, kd_ref, dec_ref, oi_ref, _), bias in zip(dirs, biases):
            y = (jnp.dot(gr_ref[rows, :].astype(BF16), w2_ref[...], preferred_element_type=F32) + bias) * LOG2E
            lg = (jnp.minimum(y, 0.0) - jnp.log2(1.0 + jnp.exp2(-jnp.abs(y)))) * (1.0 / B_GATE_TAU)
            hi = lg.astype(BF16)
            his.append(hi)
            los.append((lg - hi.astype(F32)).astype(BF16))
        bs, tots = [], []
        for n in range(len(dirs)):
            b = (jnp.dot(tris_bf[n], his[n], preferred_element_type=F32)
                 + jnp.dot(tris_bf[n], los[n], preferred_element_type=F32))
            bs.append(b)
            bc = b.reshape(GLA_BLK // B_CHUNK, B_CHUNK, B_QK)
            tots.append(jnp.broadcast_to(jnp.min(bc, axis=1, keepdims=True), bc.shape).reshape(GLA_BLK, B_QK))
        qts, kts = [], []
        for n, (q_ref, k_ref, v_ref, gr_ref, w2_ref, qt_ref, kd_ref, dec_ref, oi_ref, _) in enumerate(dirs):
            b, tot = bs[n], tots[n]
            q = q_ref[rows, :].astype(F32) * (B_DK ** -0.5)
            k = k_ref[rows, :].astype(F32)
            qt = (q * jnp.exp2(b)).astype(BF16)
            qts.append(qt)
            kts.append((k * jnp.exp2(-b)).astype(BF16))
            qt_ref[rows, :] = qt
            kd_ref[rows, :] = (k * jnp.exp2(tot - b)).astype(BF16)
            dec_ref[rows, :] = jnp.exp2(tot)
        for h in range(B_HEADS):
            ks = slice(h * B_DK, (h + 1) * B_DK)
            vs = slice(h * B_DV, (h + 1) * B_DV)
            for n, d in enumerate(dirs):
                v_ref, oi_ref = d[2], d[8]
                a = jnp.where(tris[n], _nt_dot(qts[n][:, ks], kts[n][:, ks]), 0.0).astype(BF16)
                oi_ref[rows, vs] = jnp.dot(a, v_ref[rows, vs], preferred_element_type=F32)
        return carry
    lax.fori_loop(0, GLA_TILE // GLA_BLK, body, 0)


def _gla_scan_chunks(chunks):
    work = [(c, h) for h in range(B_HEADS) for c in chunks]
    sts = [c[7][h] for c, h in work]
    for (c, h), st in zip(work, sts):
        r, v_ref, qt_ref, kd_ref, dec_ref, oi_ref, o_ref, st_ref = c
        rows = pl.ds(r, B_CHUNK)
        ks = slice(h * B_DK, (h + 1) * B_DK)
        vs = slice(h * B_DV, (h + 1) * B_DV)
        o = oi_ref[rows, vs] + _nt_dot(qt_ref[rows, ks], st.astype(BF16))
        o_ref[rows, vs] = o.astype(o_ref.dtype)
    for (c, h), st in zip(work, sts):
        r, v_ref, qt_ref, kd_ref, dec_ref, oi_ref, o_ref, st_ref = c
        rows = pl.ds(r, B_CHUNK)
        ks = slice(h * B_DK, (h + 1) * B_DK)
        vs = slice(h * B_DV, (h + 1) * B_DV)
        st_ref[h] = st * dec_ref[pl.ds(r, 1), ks] + _tn_dot(v_ref[rows, vs], kd_ref[rows, ks])


def _gla_kernel(seqs, qf_ref, kf_ref, vf_ref, grf_ref, qb_ref, kb_ref, vb_ref, grb_ref, w2f_ref, w2b_ref,
                bias_ref, of_ref, ob_ref, sf_ref, sb_ref, qtf, kdf, decf, oif, qtb, kdb, decb, oib):
    i = pl.program_id(0)
    nt = pl.num_programs(0)
    f_start = i * GLA_TILE
    b_start = (nt - 1 - i) * GLA_TILE
    f_lo, _ = _seq_bounds(f_start, seqs)
    _, b_hi = _seq_bounds(b_start, seqs)

    @pl.when(f_start == f_lo)
    def _():
        sf_ref[...] = jnp.zeros_like(sf_ref)

    @pl.when(b_start + GLA_TILE == b_hi)
    def _():
        sb_ref[...] = jnp.zeros_like(sb_ref)

    _gla_prepare([(qf_ref, kf_ref, vf_ref, grf_ref, w2f_ref, qtf, kdf, decf, oif, True),
                  (qb_ref, kb_ref, vb_ref, grb_ref, w2b_ref, qtb, kdb, decb, oib, False)], bias_ref)

    nchunk = GLA_TILE // B_CHUNK

    def body(c, carry):
        rf = pl.multiple_of(c * B_CHUNK, B_CHUNK)
        rb = pl.multiple_of((nchunk - 1 - c) * B_CHUNK, B_CHUNK)
        _gla_scan_chunks([(rf, vf_ref, qtf, kdf, decf, oif, of_ref, sf_ref),
                          (rb, vb_ref, qtb, kdb, decb, oib, ob_ref, sb_ref)])
        return carry
    lax.fori_loop(0, nchunk, body, 0, unroll=2)


def _gla(proj, gr, w2f, w2b, bias, seqs):
    t = proj.shape[0]
    nt = t // GLA_TILE
    fwd = lambda col: (lambda i: (i, col))
    bwd = lambda col: (lambda i: (nt - 1 - i, col))
    qblk, kblk, vblk = P_B_Q // B_QK, P_B_K // B_QK, P_B_V // B_V

    def specs(mk):
        return [pl.BlockSpec((GLA_TILE, B_QK), mk(qblk)), pl.BlockSpec((GLA_TILE, B_QK), mk(kblk)),
                pl.BlockSpec((GLA_TILE, B_V), mk(vblk)), pl.BlockSpec((GLA_TILE, GR_PAD), mk(0))]

    const = lambda i: (0, 0)
    state = pltpu.VMEM((B_HEADS, B_DV, B_DK), F32)
    per_dir = [pltpu.VMEM((GLA_TILE, B_QK), BF16), pltpu.VMEM((GLA_TILE, B_QK), BF16),
               pltpu.VMEM((GLA_TILE, B_QK), F32), pltpu.VMEM((GLA_TILE, B_V), F32)]
    return pl.pallas_call(
        functools.partial(_gla_kernel, seqs),
        grid=(nt,),
        in_specs=specs(fwd) + specs(bwd) + [
            pl.BlockSpec((GR_PAD, B_QK), const), pl.BlockSpec((GR_PAD, B_QK), const),
            pl.BlockSpec((2, B_QK), const)],
        out_specs=[pl.BlockSpec((GLA_TILE, B_V), fwd(0)), pl.BlockSpec((GLA_TILE, B_V), bwd(0))],
        out_shape=[jax.ShapeDtypeStruct((t, B_V), BF16), jax.ShapeDtypeStruct((t, B_V), BF16)],
        scratch_shapes=[state, state] + per_dir + per_dir,
        compiler_params=_cparams(("arbitrary",)),
        name="gla",
    )(proj, proj, proj, gr, proj, proj, proj, gr, w2f, w2b, bias)


NA_PATTERNS = 3
_NA_ROWS_FOR_TABLE = 32
_NA_PATTERN_ROWS = (0, 2 * NA_Q_ROWS, _NA_ROWS_FOR_TABLE - NA_Q_ROWS)


def _na_bias_table(rpb):
    R = _NA_ROWS_FOR_TABLE
    kr = C_WIN_R
    key_rows = NA_KEY_ROWS
    rpb = rpb.astype(F32) * LOG2E
    n_dr = rpb.shape[1]
    edge = GRID_W - C_WIN_C
    ext = jnp.concatenate([jnp.repeat(rpb[:, :, :1], edge, axis=2), rpb, jnp.repeat(rpb[:, :, -1:], edge + 1, axis=2)], axis=2)
    skew = jnp.tile(ext, (1, 1, GRID_W))[:, :, :GRID_W * (2 * GRID_W - 1)].reshape(C_HEADS, n_dr, GRID_W, 2 * GRID_W - 1)
    col = skew[:, :, :, GRID_W - 1:]
    cq = np.arange(GRID_W)[:, None]
    ck = np.arange(GRID_W)[None, :]
    col_start = np.clip(cq - C_WIN_C // 2, 0, GRID_W - C_WIN_C)
    col_ok = (ck >= col_start) & (ck < col_start + C_WIN_C)
    col = jnp.where(col_ok[None, None], col, NEG_INF)
    pieces = []
    for r in _NA_PATTERN_ROWS:
        start = int(np.clip(r - kr // 2, 0, R - key_rows))
        for jj in range(NA_Q_ROWS):
            qrow = r + jj
            ws = int(np.clip(qrow - kr // 2, 0, R - kr))
            lead, a0 = ws - start, ws - qrow + (C_WIN_R - 1)
            assert 0 <= lead <= key_rows - kr and 0 <= a0 <= n_dr - kr
            pieces.append(jnp.pad(col[:, a0:a0 + kr], ((0, 0), (lead, key_rows - kr - lead), (0, 0), (0, 0)),
                                  constant_values=NEG_INF))
    tab = jnp.stack(pieces, axis=1)
    tab = tab.reshape(C_HEADS, NA_PATTERNS, NA_Q_ROWS, key_rows, GRID_W, GRID_W).transpose(0, 1, 2, 4, 3, 5)
    return tab.reshape(C_HEADS, NA_PATTERNS, NA_Q, NA_KEYS)


def _na_kernel(seqs, q_ref, kp_ref, kc_ref, kn_ref, vp_ref, vc_ref, vn_ref, tab_ref, o_ref, kbuf, vbuf):
    tile_start = pl.program_id(1) * TILE
    seq_lo, seq_hi = _seq_bounds(tile_start, seqs)
    r0 = lax.div(tile_start, GRID_W)
    row_lo = lax.div(seq_lo, GRID_W)
    row_hi = lax.div(seq_hi, GRID_W)
    kbuf[0:NA_HALO, :] = kp_ref[...]
    kbuf[NA_HALO:NA_HALO + TILE, :] = kc_ref[...]
    kbuf[NA_HALO + TILE:, :] = kn_ref[...]
    vbuf[0:NA_HALO, :] = vp_ref[...]
    vbuf[NA_HALO:NA_HALO + TILE, :] = vc_ref[...]
    vbuf[NA_HALO + TILE:, :] = vn_ref[...]
    qk_scale = HEAD_DIM ** -0.5 * LOG2E

    def body(it, carry):
        qs, off, pid = [], [], []
        for u in range(NA_UNROLL):
            grp = it * NA_UNROLL + u
            r = r0 + NA_Q_ROWS * grp
            start = jnp.clip(r - C_WIN_R // 2, row_lo, row_hi - NA_KEY_ROWS)
            off.append(pl.multiple_of((start - r0 + NA_HALO // GRID_W) * GRID_W, NA_Q))
            pid.append(jnp.where(r == row_lo, 0, jnp.where(r == row_hi - NA_Q_ROWS, 2, 1)))
            qs.append(pl.multiple_of(grp * NA_Q, NA_Q))
        t = [_nt_dot(q_ref[pl.ds(qs[u], NA_Q), :], kbuf[pl.ds(off[u], NA_KEYS), :]) * qk_scale + tab_ref[0, pid[u]]
             for u in range(NA_UNROLL)]
        p, den = [], []
        for u in range(NA_UNROLL):
            e = jnp.exp2(t[u] - jnp.max(t[u], axis=-1, keepdims=True))
            den.append(jnp.sum(e, axis=-1, keepdims=True))
            p.append(e.astype(BF16))
        for u in range(NA_UNROLL):
            o = jnp.dot(p[u], vbuf[pl.ds(off[u], NA_KEYS), :], preferred_element_type=F32) / den[u]
            o_ref[pl.ds(qs[u], NA_Q), :] = o.astype(o_ref.dtype)
        return carry
    lax.fori_loop(0, TILE // (NA_Q * NA_UNROLL), body, 0)


def _na_attn(proj, table, seqs):
    t = proj.shape[0]
    nt = t // TILE
    per = TILE // NA_HALO
    last = t // NA_HALO - 1
    qblk, kblk, vblk = P_C_Q // HEAD_DIM, P_C_K // HEAD_DIM, P_C_V // HEAD_DIM
    prev = lambda c0: (lambda h, g: (jnp.maximum(g * per - 1, 0), c0 + h))
    cur = lambda c0: (lambda h, g: (g, c0 + h))
    nxt = lambda c0: (lambda h, g: (jnp.minimum((g + 1) * per, last), c0 + h))
    return pl.pallas_call(
        functools.partial(_na_kernel, seqs),
        grid=(C_HEADS, nt),
        in_specs=[
            pl.BlockSpec((TILE, HEAD_DIM), cur(qblk)),
            pl.BlockSpec((NA_HALO, HEAD_DIM), prev(kblk)),
            pl.BlockSpec((TILE, HEAD_DIM), cur(kblk)),
            pl.BlockSpec((NA_HALO, HEAD_DIM), nxt(kblk)),
            pl.BlockSpec((NA_HALO, HEAD_DIM), prev(vblk)),
            pl.BlockSpec((TILE, HEAD_DIM), cur(vblk)),
            pl.BlockSpec((NA_HALO, HEAD_DIM), nxt(vblk)),
            pl.BlockSpec((1, NA_PATTERNS, NA_Q, NA_KEYS), lambda h, g: (h, 0, 0, 0)),
        ],
        out_specs=pl.BlockSpec((TILE, HEAD_DIM), lambda h, g: (g, h)),
        out_shape=jax.ShapeDtypeStruct((t, C_W), BF16),
        scratch_shapes=[pltpu.VMEM((TILE + 2 * NA_HALO, HEAD_DIM), BF16),
                        pltpu.VMEM((TILE + 2 * NA_HALO, HEAD_DIM), BF16)],
        compiler_params=_cparams(("parallel", "parallel")),
        name="na_attn",
    )(proj, proj, proj, proj, proj, proj, proj, table)


def _pack_w_in(w):
    sl = lambda off, n: w[:, off:off + n]
    cols = [sl(IN_A_Q, A_Q), sl(IN_B_V, B_V), sl(IN_B_OG, B_V), sl(IN_C_Q, C_W), sl(IN_C_K, C_W), sl(IN_C_V, C_W),
            sl(IN_GL, 3 * D_MODEL), sl(IN_B_Q, B_QK), sl(IN_B_K, B_QK), sl(IN_A_K, A_KV), sl(IN_A_V, A_KV)]
    wp = jnp.concatenate(cols, axis=1).astype(BF16)
    wgr = jnp.pad(sl(IN_B_GR, 2 * B_GATE_RANK), ((0, 0), (0, GR_PAD - 2 * B_GATE_RANK))).astype(BF16)
    return wp, wgr


def _pack_gla_w2(w2):
    z = jnp.zeros((GR_PAD, B_QK), F32)
    w2f = z.at[0:B_GATE_RANK].set(w2[0]).astype(BF16)
    w2b = z.at[B_GATE_RANK:2 * B_GATE_RANK].set(w2[1]).astype(BF16)
    return w2f, w2b


def kernel(x_prompt, x_sample, norm1, w_in, sink_a, gla_w2, gla_b, gla_norm, rpb_c, w_br_a, w_br_b, w_br_c, w_out,
           norm2, w_ffn_in, w_ffn_out, norm_f):
    bp, sp, d = x_prompt.shape
    bs, ss, _ = x_sample.shape
    tp, ts = bp * sp, bs * ss
    assert sp % TILE == 0 and ss % TILE == 0 and d == D_MODEL
    seqs = (tp, sp, ss)
    x = jnp.concatenate([x_prompt.reshape(tp, d), x_sample.reshape(ts, d)], axis=0)

    for l in range(DEPTH):
        wp, wgr = _pack_w_in(w_in[l])
        w2f, w2b = _pack_gla_w2(gla_w2[l])
        table = _na_bias_table(rpb_c[l])
        proj, gr = _inproj(x, norm1[l].reshape(1, d), wp, wgr)
        o_a = _win_attn(proj, sink_a[l], seqs)
        o_f, o_bw = _gla(proj, gr, w2f, w2b, gla_b[l], seqs)
        o_c = _na_attn(proj, table, seqs)
        merged = _merge(o_a, o_c, o_f, o_bw, proj, gla_norm[l].reshape(1, B_DV),
                        w_br_a[l].astype(BF16), w_br_b[l].astype(BF16), w_br_c[l].astype(BF16))
        x = _mm_res(merged, w_out[l].astype(BF16), x, 1024, 1024, "out_proj")
        act = _ffn_in(x, norm2[l].reshape(1, d), w_ffn_in[l].astype(BF16))
        x = _mm_res(act, w_ffn_out[l].astype(BF16), x, 1024, 512, "ffn_out")

    g = norm_f.reshape(1, d)
    y_p = _final_norm(x, g, 0, tp).reshape(bp, sp, d)
    y_s = _final_norm(x, g, tp, ts).reshape(bs, ss, d)
    return (y_p, y_s)
```

```python
import functools

import numpy as np
import jax
import jax.numpy as jnp
from jax import lax
from jax.experimental import pallas as pl
from jax.experimental.pallas import tpu as pltpu

F32 = jnp.float32
BF16 = jnp.bfloat16

D_MODEL = 2048
DEPTH = 2
GRID_W = 64
HEAD_DIM = 128
A_HEADS = 8
A_KV_HEADS = 2
A_GROUP = A_HEADS // A_KV_HEADS
A_WINDOW = 128
A_BLOCK = 128
B_HEADS = 4
B_DK = 128
B_DV = 256
B_CHUNK = 64
B_GATE_RANK = 16
B_GATE_TAU = 16.0
C_HEADS = 8
C_WIN_R = 8
C_WIN_C = 16
D_FF = -(-8 * D_MODEL // (3 * 256)) * 256
RMS_EPS = 1e-6
NEG_INF = -1e30
LOG2E = 1.4426950408889634

A_Q = A_HEADS * HEAD_DIM
A_KV = A_KV_HEADS * HEAD_DIM
B_QK = B_HEADS * B_DK
B_V = B_HEADS * B_DV
C_W = C_HEADS * HEAD_DIM
SPLIT_SIZES = (A_Q, A_KV, A_KV, B_QK, B_QK, B_V, B_V, 2 * B_GATE_RANK, C_W, C_W, C_W, 3 * D_MODEL)
SPLIT_OFF = tuple(int(i) for i in np.cumsum((0,) + SPLIT_SIZES))
(IN_A_Q, IN_A_K, IN_A_V, IN_B_Q, IN_B_K, IN_B_V, IN_B_OG, IN_B_GR, IN_C_Q, IN_C_K, IN_C_V, IN_GL) = SPLIT_OFF[:-1]

P_A_Q = 0
P_B_V = 1024
P_B_OG = 2048
P_C_Q = 3072
P_C_K = 4096
P_C_V = 5120
P_B_Q = 6144
P_B_K = 6656
P_A_K = 7168
P_A_V = 7424
P_COLS = 7680
GR_PAD = 128

TILE = 2048
NA_Q_ROWS = 4
NA_KEY_ROWS = 12
NA_Q = NA_Q_ROWS * GRID_W
NA_KEYS = NA_KEY_ROWS * GRID_W
NA_HALO = 256
NA_UNROLL = 4

VMEM_LIMIT = 56 * 1024 * 1024


def _cparams(sem):
    return pltpu.CompilerParams(dimension_semantics=sem, vmem_limit_bytes=VMEM_LIMIT)


def _seq_bounds(pos, seqs):
    tp, sp, ss = seqs
    in_p = pos < tp
    lo_p = lax.div(pos, sp) * sp
    lo_s = tp + lax.div(jnp.maximum(pos - tp, 0), ss) * ss
    lo = jnp.where(in_p, lo_p, lo_s)
    hi = lo + jnp.where(in_p, sp, ss)
    return lo, hi


NORM_ROWS = 256


def _rmsnorm_rows(x_ref, g_ref, h_ref):
    def body(c, carry):
        r = pl.multiple_of(c * NORM_ROWS, NORM_ROWS)
        xs = x_ref[pl.ds(r, NORM_ROWS), :]
        ms = jnp.mean(xs * xs, axis=-1, keepdims=True)
        h_ref[pl.ds(r, NORM_ROWS), :] = (xs * lax.rsqrt(ms + RMS_EPS) * g_ref[...]).astype(h_ref.dtype)
        return carry
    lax.fori_loop(0, x_ref.shape[0] // NORM_ROWS, body, 0)


def _inproj_kernel(x_ref, g_ref, w_ref, wgr_ref, proj_ref, gr_ref, h_ref):
    @pl.when(pl.program_id(1) == 0)
    def _():
        _rmsnorm_rows(x_ref, g_ref, h_ref)
        gr_ref[...] = jnp.dot(h_ref[...], wgr_ref[...], preferred_element_type=F32)
    proj_ref[...] = jnp.dot(h_ref[...], w_ref[...], preferred_element_type=F32).astype(proj_ref.dtype)


def _inproj(x, g, w, wgr, bm=1024, bn=1536):
    t, d = x.shape
    n = w.shape[1]
    return pl.pallas_call(
        _inproj_kernel,
        grid=(t // bm, n // bn),
        in_specs=[
            pl.BlockSpec((bm, d), lambda i, j: (i, 0)),
            pl.BlockSpec((1, d), lambda i, j: (0, 0)),
            pl.BlockSpec((d, bn), lambda i, j: (0, j)),
            pl.BlockSpec((d, GR_PAD), lambda i, j: (0, 0)),
        ],
        out_specs=[
            pl.BlockSpec((bm, bn), lambda i, j: (i, j)),
            pl.BlockSpec((bm, GR_PAD), lambda i, j: (i, 0)),
            pl.BlockSpec((bm, d), lambda i, j: (i, 0)),
        ],
        out_shape=[jax.ShapeDtypeStruct((t, n), BF16), jax.ShapeDtypeStruct((t, GR_PAD), F32),
                   jax.ShapeDtypeStruct((t, d), BF16)],
        compiler_params=_cparams(("parallel", "arbitrary")),
        name="inproj",
    )(x, g, w, wgr)


def _ffn_in_kernel(h_ref, wg_ref, wu_ref, act_ref):
    h = h_ref[...]
    gate = jnp.dot(h, wg_ref[...], preferred_element_type=F32)
    up = jnp.dot(h, wu_ref[...], preferred_element_type=F32)
    act_ref[...] = (gate * jax.nn.sigmoid(gate) * up).astype(act_ref.dtype)


def _ffn_in(h, w, bm=1024, bn=512):
    t, d = h.shape
    ff = w.shape[1] // 2
    nj = ff // bn
    return pl.pallas_call(
        _ffn_in_kernel,
        grid=(t // bm, nj),
        in_specs=[
            pl.BlockSpec((bm, d), lambda i, j: (i, 0)),
            pl.BlockSpec((d, bn), lambda i, j: (0, j)),
            pl.BlockSpec((d, bn), lambda i, j: (0, nj + j)),
        ],
        out_specs=pl.BlockSpec((bm, bn), lambda i, j: (i, j)),
        out_shape=jax.ShapeDtypeStruct((t, ff), BF16),
        compiler_params=_cparams(("parallel", "arbitrary")),
        name="ffn_in",
    )(h, w, w)


def _out_proj_kernel(a_ref, w_ref, x_ref, g_ref, xo_ref, h_ref):
    xo_ref[...] = x_ref[...] + jnp.dot(a_ref[...], w_ref[...], preferred_element_type=F32)
    _rmsnorm_rows(xo_ref, g_ref, h_ref)


def _out_proj(a, w, x, g, bm=512):
    t, k = a.shape
    d = w.shape[1]
    return pl.pallas_call(
        _out_proj_kernel,
        grid=(t // bm,),
        in_specs=[
            pl.BlockSpec((bm, k), lambda i: (i, 0)),
            pl.BlockSpec((k, d), lambda i: (0, 0)),
            pl.BlockSpec((bm, d), lambda i: (i, 0)),
            pl.BlockSpec((1, d), lambda i: (0, 0)),
        ],
        out_specs=[pl.BlockSpec((bm, d), lambda i: (i, 0)), pl.BlockSpec((bm, d), lambda i: (i, 0))],
        out_shape=[jax.ShapeDtypeStruct((t, d), F32), jax.ShapeDtypeStruct((t, d), BF16)],
        compiler_params=_cparams(("parallel",)),
        name="out_proj",
    )(a, w, x, g)


def _mm_res_kernel(a_ref, w_ref, x_ref, o_ref):
    o_ref[...] = x_ref[...] + jnp.dot(a_ref[...], w_ref[...], preferred_element_type=F32)


def _mm_res(a, w, x, bm, bn, name):
    t, k = a.shape
    n = w.shape[1]
    return pl.pallas_call(
        _mm_res_kernel,
        grid=(t // bm, n // bn),
        in_specs=[
            pl.BlockSpec((bm, k), lambda i, j: (i, 0)),
            pl.BlockSpec((k, bn), lambda i, j: (0, j)),
            pl.BlockSpec((bm, bn), lambda i, j: (i, j)),
        ],
        out_specs=pl.BlockSpec((bm, bn), lambda i, j: (i, j)),
        out_shape=jax.ShapeDtypeStruct((t, n), F32),
        compiler_params=_cparams(("parallel", "arbitrary")),
        name=name,
    )(a, w, x)


MERGE_ROWS = 256


def _merge_kernel(oa_ref, oc_ref, of_ref, obw_ref, og_ref, gain_ref, h_ref, wga_ref, wgb_ref, wgc_ref,
                  wa_ref, wb_ref, wc_ref, out_ref, ob_ref):
    @pl.when(pl.program_id(1) == 0)
    def _():
        def body(c, carry):
            r = pl.multiple_of(c * MERGE_ROWS, MERGE_ROWS)
            o = of_ref[pl.ds(r, MERGE_ROWS), :].astype(F32) + obw_ref[pl.ds(r, MERGE_ROWS), :].astype(F32)
            og = og_ref[pl.ds(r, MERGE_ROWS), :].astype(F32)
            for h in range(B_HEADS):
                sl = slice(h * B_DV, (h + 1) * B_DV)
                oh = o[:, sl]
                ms = jnp.mean(oh * oh, axis=-1, keepdims=True)
                ogh = og[:, sl]
                y = oh * lax.rsqrt(ms + RMS_EPS) * gain_ref[...] * (ogh * jax.nn.sigmoid(ogh))
                ob_ref[pl.ds(r, MERGE_ROWS), sl] = y.astype(ob_ref.dtype)
            return carry
        lax.fori_loop(0, of_ref.shape[0] // MERGE_ROWS, body, 0)

    h = h_ref[...]
    ga = jax.nn.sigmoid(jnp.dot(h, wga_ref[...], preferred_element_type=F32))
    acc = ga * jnp.dot(oa_ref[...], wa_ref[...], preferred_element_type=F32)
    gb = jax.nn.sigmoid(jnp.dot(h, wgb_ref[...], preferred_element_type=F32))
    acc = acc + gb * jnp.dot(ob_ref[...], wb_ref[...], preferred_element_type=F32)
    gc = jax.nn.sigmoid(jnp.dot(h, wgc_ref[...], preferred_element_type=F32))
    acc = acc + gc * jnp.dot(oc_ref[...], wc_ref[...], preferred_element_type=F32)
    out_ref[...] = acc.astype(out_ref.dtype)


def _merge(o_a, o_c, o_f, o_bw, proj, h, gain, wgate, wa, wb, wc, bm=1024, bn=256):
    t = o_a.shape[0]
    d = wa.shape[1]
    kb = o_a.shape[1]
    gstep = d // bn
    wide = lambda i, j: (i, 0)
    return pl.pallas_call(
        _merge_kernel,
        grid=(t // bm, d // bn),
        in_specs=[
            pl.BlockSpec((bm, kb), wide),
            pl.BlockSpec((bm, kb), wide),
            pl.BlockSpec((bm, kb), wide),
            pl.BlockSpec((bm, kb), wide),
            pl.BlockSpec((bm, B_V), lambda i, j: (i, P_B_OG // B_V)),
            pl.BlockSpec((1, B_DV), lambda i, j: (0, 0)),
            pl.BlockSpec((bm, d), wide),
            pl.BlockSpec((d, bn), lambda i, j: (0, j)),
            pl.BlockSpec((d, bn), lambda i, j: (0, gstep + j)),
            pl.BlockSpec((d, bn), lambda i, j: (0, 2 * gstep + j)),
            pl.BlockSpec((kb, bn), lambda i, j: (0, j)),
            pl.BlockSpec((kb, bn), lambda i, j: (0, j)),
            pl.BlockSpec((kb, bn), lambda i, j: (0, j)),
        ],
        out_specs=pl.BlockSpec((bm, bn), lambda i, j: (i, j)),
        out_shape=jax.ShapeDtypeStruct((t, d), BF16),
        scratch_shapes=[pltpu.VMEM((bm, kb), BF16)],
        compiler_params=_cparams(("parallel", "arbitrary")),
        name="merge",
    )(o_a, o_c, o_f, o_bw, proj, gain, h, wgate, wgate, wgate, wa, wb, wc)


def _final_norm_kernel(x_ref, g_ref, y_ref):
    _rmsnorm_rows(x_ref, g_ref, y_ref)


def _final_norm(x, g, row0, rows, bm=512):
    d = x.shape[1]
    blk0 = row0 // bm
    return pl.pallas_call(
        _final_norm_kernel,
        grid=(rows // bm,),
        in_specs=[
            pl.BlockSpec((bm, d), lambda i: (blk0 + i, 0)),
            pl.BlockSpec((1, d), lambda i: (0, 0)),
        ],
        out_specs=pl.BlockSpec((bm, d), lambda i: (i, 0)),
        out_shape=jax.ShapeDtypeStruct((rows, d), F32),
        compiler_params=_cparams(("parallel",)),
        name="final_norm",
    )(x, g)


A_KEYS = 3 * A_BLOCK
A_ROWS = A_GROUP * A_BLOCK


def _nt_dot(a, b):
    return lax.dot_general(a, b, (((1,), (1,)), ((), ())), preferred_element_type=F32)


def _win_kernel(seqs, sink_ref, q_ref, kp_ref, kc_ref, kn_ref, vp_ref, vc_ref, vn_ref, o_ref, kbuf, vbuf):
    tile_start = pl.program_id(0) * TILE
    seq_lo, seq_hi = _seq_bounds(tile_start, seqs)
    kbuf[0:A_BLOCK, :] = kp_ref[...]
    kbuf[A_BLOCK:A_BLOCK + TILE, :] = kc_ref[...]
    kbuf[A_BLOCK + TILE:, :] = kn_ref[...]
    vbuf[0:A_BLOCK, :] = vp_ref[...]
    vbuf[A_BLOCK:A_BLOCK + TILE, :] = vc_ref[...]
    vbuf[A_BLOCK + TILE:, :] = vn_ref[...]

    row = lax.broadcasted_iota(jnp.int32, (A_ROWS, A_KEYS), 0)
    col = lax.broadcasted_iota(jnp.int32, (A_ROWS, A_KEYS), 1)
    dist = jnp.abs((row % A_BLOCK) - (col - A_BLOCK))
    in_window = dist <= A_WINDOW
    distf = dist.astype(F32)
    grp = lax.broadcasted_iota(jnp.int32, (A_ROWS, 1), 0) // A_BLOCK
    kcol = lax.broadcasted_iota(jnp.int32, (1, A_KEYS), 1)
    qk_scale = HEAD_DIM ** -0.5 * LOG2E

    nbias, sink = [], []
    for kvh in range(A_KV_HEADS):
        slope = jnp.zeros((A_ROWS, 1), F32)
        snk = jnp.zeros((A_ROWS, 1), F32)
        for g in range(A_GROUP):
            h = kvh * A_GROUP + g
            slope = jnp.where(grp == g, 2.0 ** (-8.0 * (h + 1) / A_HEADS) * LOG2E, slope)
            snk = jnp.where(grp == g, sink_ref[h] * LOG2E, snk)
        nbias.append(jnp.where(in_window, -slope * distf, NEG_INF))
        sink.append(snk)

    def body(n, carry):
        r = pl.multiple_of(n * A_BLOCK, A_BLOCK)
        kabs = tile_start - A_BLOCK + r + kcol
        kbias = jnp.where((kabs >= seq_lo) & (kabs < seq_hi), 0.0, NEG_INF)
        t = []
        for kvh in range(A_KV_HEADS):
            q4 = jnp.concatenate(
                [q_ref[pl.ds(r, A_BLOCK), (kvh * A_GROUP + g) * HEAD_DIM:(kvh * A_GROUP + g + 1) * HEAD_DIM]
                 for g in range(A_GROUP)], axis=0)
            kw = kbuf[pl.ds(r, A_KEYS), kvh * HEAD_DIM:(kvh + 1) * HEAD_DIM]
            t.append(_nt_dot(q4, kw) * qk_scale + nbias[kvh] + kbias)
        p, den = [], []
        for kvh in range(A_KV_HEADS):
            m = jnp.maximum(jnp.max(t[kvh], axis=-1, keepdims=True), sink[kvh])
            e = jnp.exp2(t[kvh] - m)
            den.append(jnp.sum(e, axis=-1, keepdims=True) + jnp.exp2(sink[kvh] - m))
            p.append(e.astype(BF16))
        for kvh in range(A_KV_HEADS):
            vw = vbuf[pl.ds(r, A_KEYS), kvh * HEAD_DIM:(kvh + 1) * HEAD_DIM]
            o = jnp.dot(p[kvh], vw, preferred_element_type=F32) / den[kvh]
            for g in range(A_GROUP):
                h = kvh * A_GROUP + g
                o_ref[pl.ds(r, A_BLOCK), h * HEAD_DIM:(h + 1) * HEAD_DIM] = (
                    o[g * A_BLOCK:(g + 1) * A_BLOCK].astype(o_ref.dtype))
        return carry
    lax.fori_loop(0, TILE // A_BLOCK, body, 0)


def _win_attn(proj, sink, seqs):
    t = proj.shape[0]
    nt = t // TILE
    per = TILE // A_BLOCK
    last = t // A_BLOCK - 1
    kblk = P_A_K // A_KV
    vblk = P_A_V // A_KV
    prev = lambda col: (lambda g: (jnp.maximum(g * per - 1, 0), col))
    cur = lambda col: (lambda g: (g, col))
    nxt = lambda col: (lambda g: (jnp.minimum((g + 1) * per, last), col))
    return pl.pallas_call(
        functools.partial(_win_kernel, seqs),
        grid=(nt,),
        in_specs=[
            pl.BlockSpec(memory_space=pltpu.SMEM),
            pl.BlockSpec((TILE, A_Q), lambda g: (g, P_A_Q // A_Q)),
            pl.BlockSpec((A_BLOCK, A_KV), prev(kblk)),
            pl.BlockSpec((TILE, A_KV), cur(kblk)),
            pl.BlockSpec((A_BLOCK, A_KV), nxt(kblk)),
            pl.BlockSpec((A_BLOCK, A_KV), prev(vblk)),
            pl.BlockSpec((TILE, A_KV), cur(vblk)),
            pl.BlockSpec((A_BLOCK, A_KV), nxt(vblk)),
        ],
        out_specs=pl.BlockSpec((TILE, A_Q), lambda g: (g, 0)),
        out_shape=jax.ShapeDtypeStruct((t, A_Q), BF16),
        scratch_shapes=[pltpu.VMEM((TILE + 2 * A_BLOCK, A_KV), BF16),
                        pltpu.VMEM((TILE + 2 * A_BLOCK, A_KV), BF16)],
        compiler_params=_cparams(("parallel",)),
        name="win_attn",
    )(sink, proj, proj, proj, proj, proj, proj, proj)


GLA_TILE = 1024
GLA_BLK = 256


def _tn_dot(a, b):
    return lax.dot_general(a, b, (((0,), (0,)), ((), ())), preferred_element_type=F32)


def _gla_prepare(dirs, bias_ref):
    row = lax.broadcasted_iota(jnp.int32, (GLA_BLK, GLA_BLK), 0)
    col = lax.broadcasted_iota(jnp.int32, (GLA_BLK, GLA_BLK), 1)
    same = (row // B_CHUNK) == (col // B_CHUNK)
    tris = [same & ((row >= col) if d[-1] else (row <= col)) for d in dirs]
    tris_bf = [t.astype(BF16) for t in tris]
    biases = [bias_ref[n:n + 1, :] for n in range(len(dirs))]

    def body(i, carry):
        r = pl.multiple_of(i * GLA_BLK, GLA_BLK)
        rows = pl.ds(r, GLA_BLK)
        his, los = [], []
        for (q_ref, k_ref, v_ref, gr_ref, w2_ref, qt_ref, kd_ref, dec_ref, oi_ref, _), bias in zip(dirs, biases):
            y = (jnp.dot(gr_ref[rows, :].astype(BF16), w2_ref[...], preferred_element_type=F32) + bias) * LOG2E
            lg = (jnp.minimum(y, 0.0) - jnp.log2(1.0 + jnp.exp2(-jnp.abs(y)))) * (1.0 / B_GATE_TAU)
            hi = lg.astype(BF16)
            his.append(hi)
            los.append((lg - hi.astype(F32)).astype(BF16))
        bs, tots = [], []
        for n in range(len(dirs)):
            b = (jnp.dot(tris_bf[n], his[n], preferred_element_type=F32)
                 + jnp.dot(tris_bf[n], los[n], preferred_element_type=F32))
            bs.append(b)
            bc = b.reshape(GLA_BLK // B_CHUNK, B_CHUNK, B_QK)
            tots.append(jnp.broadcast_to(jnp.min(bc, axis=1, keepdims=True), bc.shape).reshape(GLA_BLK, B_QK))
        qts, kts = [], []
        for n, (q_ref, k_ref, v_ref, gr_ref, w2_ref, qt_ref, kd_ref, dec_ref, oi_ref, _) in enumerate(dirs):
            b, tot = bs[n], tots[n]
            q = q_ref[rows, :].astype(F32) * (B_DK ** -0.5)
            k = k_ref[rows, :].astype(F32)
            qt = (q * jnp.exp2(b)).astype(BF16)
            qts.append(qt)
            kts.append((k * jnp.exp2(-b)).astype(BF16))
            qt_ref[rows, :] = qt
            kd_ref[rows, :] = (k * jnp.exp2(tot - b)).astype(BF16)
            dec_ref[rows, :] = jnp.exp2(tot)
        for h in range(B_HEADS):
            ks = slice(h * B_DK, (h + 1) * B_DK)
            vs = slice(h * B_DV, (h + 1) * B_DV)
            for n, d in enumerate(dirs):
                v_ref, oi_ref = d[2], d[8]
                a = jnp.where(tris[n], _nt_dot(qts[n][:, ks], kts[n][:, ks]), 0.0).astype(BF16)
                oi_ref[rows, vs] = jnp.dot(a, v_ref[rows, vs], preferred_element_type=F32)
        return carry
    lax.fori_loop(0, GLA_TILE // GLA_BLK, body, 0)


def _gla_scan_chunks(chunks):
    work = [(c, h) for h in range(B_HEADS) for c in chunks]
    sts = [c[7][h] for c, h in work]
    for (c, h), st in zip(work, sts):
        r, v_ref, qt_ref, kd_ref, dec_ref, oi_ref, o_ref, st_ref = c
        rows = pl.ds(r, B_CHUNK)
        ks = slice(h * B_DK, (h + 1) * B_DK)
        vs = slice(h * B_DV, (h + 1) * B_DV)
        o = oi_ref[rows, vs] + _nt_dot(qt_ref[rows, ks], st.astype(BF16))
        o_ref[rows, vs] = o.astype(o_ref.dtype)
    for (c, h), st in zip(work, sts):
        r, v_ref, qt_ref, kd_ref, dec_ref, oi_ref, o_ref, st_ref = c
        rows = pl.ds(r, B_CHUNK)
        ks = slice(h * B_DK, (h + 1) * B_DK)
        vs = slice(h * B_DV, (h + 1) * B_DV)
        st_ref[h] = st * dec_ref[pl.ds(r, 1), ks] + _tn_dot(v_ref[rows, vs], kd_ref[rows, ks])


def _gla_kernel(seqs, qf_ref, kf_ref, vf_ref, grf_ref, qb_ref, kb_ref, vb_ref, grb_ref, w2f_ref, w2b_ref,
                bias_ref, of_ref, ob_ref, sf_ref, sb_ref, qtf, kdf, decf, oif, qtb, kdb, decb, oib):
    i = pl.program_id(0)
    nt = pl.num_programs(0)
    f_start = i * GLA_TILE
    b_start = (nt - 1 - i) * GLA_TILE
    f_lo, _ = _seq_bounds(f_start, seqs)
    _, b_hi = _seq_bounds(b_start, seqs)

    @pl.when(f_start == f_lo)
    def _():
        sf_ref[...] = jnp.zeros_like(sf_ref)

    @pl.when(b_start + GLA_TILE == b_hi)
    def _():
        sb_ref[...] = jnp.zeros_like(sb_ref)

    _gla_prepare([(qf_ref, kf_ref, vf_ref, grf_ref, w2f_ref, qtf, kdf, decf, oif, True),
                  (qb_ref, kb_ref, vb_ref, grb_ref, w2b_ref, qtb, kdb, decb, oib, False)], bias_ref)

    nchunk = GLA_TILE // B_CHUNK

    def body(c, carry):
        rf = pl.multiple_of(c * B_CHUNK, B_CHUNK)
        rb = pl.multiple_of((nchunk - 1 - c) * B_CHUNK, B_CHUNK)
        _gla_scan_chunks([(rf, vf_ref, qtf, kdf, decf, oif, of_ref, sf_ref),
                          (rb, vb_ref, qtb, kdb, decb, oib, ob_ref, sb_ref)])
        return carry
    lax.fori_loop(0, nchunk, body, 0, unroll=2)


def _gla(proj, gr, w2f, w2b, bias, seqs):
    t = proj.shape[0]
    nt = t // GLA_TILE
    fwd = lambda col: (lambda i: (i, col))
    bwd = lambda col: (lambda i: (nt - 1 - i, col))
    qblk, kblk, vblk = P_B_Q // B_QK, P_B_K // B_QK, P_B_V // B_V

    def specs(mk):
        return [pl.BlockSpec((GLA_TILE, B_QK), mk(qblk)), pl.BlockSpec((GLA_TILE, B_QK), mk(kblk)),
                pl.BlockSpec((GLA_TILE, B_V), mk(vblk)), pl.BlockSpec((GLA_TILE, GR_PAD), mk(0))]

    const = lambda i: (0, 0)
    state = pltpu.VMEM((B_HEADS, B_DV, B_DK), F32)
    per_dir = [pltpu.VMEM((GLA_TILE, B_QK), BF16), pltpu.VMEM((GLA_TILE, B_QK), BF16),
               pltpu.VMEM((GLA_TILE, B_QK), F32), pltpu.VMEM((GLA_TILE, B_V), F32)]
    return pl.pallas_call(
        functools.partial(_gla_kernel, seqs),
        grid=(nt,),
        in_specs=specs(fwd) + specs(bwd) + [
            pl.BlockSpec((GR_PAD, B_QK), const), pl.BlockSpec((GR_PAD, B_QK), const),
            pl.BlockSpec((2, B_QK), const)],
        out_specs=[pl.BlockSpec((GLA_TILE, B_V), fwd(0)), pl.BlockSpec((GLA_TILE, B_V), bwd(0))],
        out_shape=[jax.ShapeDtypeStruct((t, B_V), BF16), jax.ShapeDtypeStruct((t, B_V), BF16)],
        scratch_shapes=[state, state] + per_dir + per_dir,
        compiler_params=_cparams(("arbitrary",)),
        name="gla",
    )(proj, proj, proj, gr, proj, proj, proj, gr, w2f, w2b, bias)


NA_PATTERNS = 3
_NA_ROWS_FOR_TABLE = 32
_NA_PATTERN_ROWS = (0, 2 * NA_Q_ROWS, _NA_ROWS_FOR_TABLE - NA_Q_ROWS)


def _na_bias_table(rpb):
    R = _NA_ROWS_FOR_TABLE
    kr = C_WIN_R
    key_rows = NA_KEY_ROWS
    rpb = rpb.astype(F32) * LOG2E
    n_dr = rpb.shape[1]
    edge = GRID_W - C_WIN_C
    ext = jnp.concatenate([jnp.repeat(rpb[:, :, :1], edge, axis=2), rpb, jnp.repeat(rpb[:, :, -1:], edge + 1, axis=2)], axis=2)
    skew = jnp.tile(ext, (1, 1, GRID_W))[:, :, :GRID_W * (2 * GRID_W - 1)].reshape(C_HEADS, n_dr, GRID_W, 2 * GRID_W - 1)
    col = skew[:, :, :, GRID_W - 1:]
    cq = np.arange(GRID_W)[:, None]
    ck = np.arange(GRID_W)[None, :]
    col_start = np.clip(cq - C_WIN_C // 2, 0, GRID_W - C_WIN_C)
    col_ok = (ck >= col_start) & (ck < col_start + C_WIN_C)
    col = jnp.where(col_ok[None, None], col, NEG_INF)
    pieces = []
    for r in _NA_PATTERN_ROWS:
        start = int(np.clip(r - kr // 2, 0, R - key_rows))
        for jj in range(NA_Q_ROWS):
            qrow = r + jj
            ws = int(np.clip(qrow - kr // 2, 0, R - kr))
            lead, a0 = ws - start, ws - qrow + (C_WIN_R - 1)
            assert 0 <= lead <= key_rows - kr and 0 <= a0 <= n_dr - kr
            pieces.append(jnp.pad(col[:, a0:a0 + kr], ((0, 0), (lead, key_rows - kr - lead), (0, 0), (0, 0)),
                                  constant_values=NEG_INF))
    tab = jnp.stack(pieces, axis=1)
    tab = tab.reshape(C_HEADS, NA_PATTERNS, NA_Q_ROWS, key_rows, GRID_W, GRID_W).transpose(0, 1, 2, 4, 3, 5)
    return tab.reshape(C_HEADS, NA_PATTERNS, NA_Q, NA_KEYS)


def _na_kernel(seqs, q_ref, kp_ref, kc_ref, kn_ref, vp_ref, vc_ref, vn_ref, tab_ref, o_ref, kbuf, vbuf):
    tile_start = pl.program_id(1) * TILE
    seq_lo, seq_hi = _seq_bounds(tile_start, seqs)
    r0 = lax.div(tile_start, GRID_W)
    row_lo = lax.div(seq_lo, GRID_W)
    row_hi = lax.div(seq_hi, GRID_W)
    kbuf[0:NA_HALO, :] = kp_ref[...]
    kbuf[NA_HALO:NA_HALO + TILE, :] = kc_ref[...]
    kbuf[NA_HALO + TILE:, :] = kn_ref[...]
    vbuf[0:NA_HALO, :] = vp_ref[...]
    vbuf[NA_HALO:NA_HALO + TILE, :] = vc_ref[...]
    vbuf[NA_HALO + TILE:, :] = vn_ref[...]
    qk_scale = HEAD_DIM ** -0.5 * LOG2E

    def body(it, carry):
        qs, off, pid = [], [], []
        for u in range(NA_UNROLL):
            grp = it * NA_UNROLL + u
            r = r0 + NA_Q_ROWS * grp
            start = jnp.clip(r - C_WIN_R // 2, row_lo, row_hi - NA_KEY_ROWS)
            off.append(pl.multiple_of((start - r0 + NA_HALO // GRID_W) * GRID_W, NA_Q))
            pid.append(jnp.where(r == row_lo, 0, jnp.where(r == row_hi - NA_Q_ROWS, 2, 1)))
            qs.append(pl.multiple_of(grp * NA_Q, NA_Q))
        t = [_nt_dot(q_ref[pl.ds(qs[u], NA_Q), :], kbuf[pl.ds(off[u], NA_KEYS), :]) * qk_scale + tab_ref[0, pid[u]]
             for u in range(NA_UNROLL)]
        p, den = [], []
        for u in range(NA_UNROLL):
            e = jnp.exp2(t[u] - jnp.max(t[u], axis=-1, keepdims=True))
            den.append(jnp.sum(e, axis=-1, keepdims=True))
            p.append(e.astype(BF16))
        for u in range(NA_UNROLL):
            o = jnp.dot(p[u], vbuf[pl.ds(off[u], NA_KEYS), :], preferred_element_type=F32) / den[u]
            o_ref[pl.ds(qs[u], NA_Q), :] = o.astype(o_ref.dtype)
        return carry
    lax.fori_loop(0, TILE // (NA_Q * NA_UNROLL), body, 0)


def _na_attn(proj, table, seqs):
    t = proj.shape[0]
    nt = t // TILE
    per = TILE // NA_HALO
    last = t // NA_HALO - 1
    qblk, kblk, vblk = P_C_Q // HEAD_DIM, P_C_K // HEAD_DIM, P_C_V // HEAD_DIM
    prev = lambda c0: (lambda h, g: (jnp.maximum(g * per - 1, 0), c0 + h))
    cur = lambda c0: (lambda h, g: (g, c0 + h))
    nxt = lambda c0: (lambda h, g: (jnp.minimum((g + 1) * per, last), c0 + h))
    return pl.pallas_call(
        functools.partial(_na_kernel, seqs),
        grid=(C_HEADS, nt),
        in_specs=[
            pl.BlockSpec((TILE, HEAD_DIM), cur(qblk)),
            pl.BlockSpec((NA_HALO, HEAD_DIM), prev(kblk)),
            pl.BlockSpec((TILE, HEAD_DIM), cur(kblk)),
            pl.BlockSpec((NA_HALO, HEAD_DIM), nxt(kblk)),
            pl.BlockSpec((NA_HALO, HEAD_DIM), prev(vblk)),
            pl.BlockSpec((TILE, HEAD_DIM), cur(vblk)),
            pl.BlockSpec((NA_HALO, HEAD_DIM), nxt(vblk)),
            pl.BlockSpec((1, NA_PATTERNS, NA_Q, NA_KEYS), lambda h, g: (h, 0, 0, 0)),
        ],
        out_specs=pl.BlockSpec((TILE, HEAD_DIM), lambda h, g: (g, h)),
        out_shape=jax.ShapeDtypeStruct((t, C_W), BF16),
        scratch_shapes=[pltpu.VMEM((TILE + 2 * NA_HALO, HEAD_DIM), BF16),
                        pltpu.VMEM((TILE + 2 * NA_HALO, HEAD_DIM), BF16)],
        compiler_params=_cparams(("parallel", "parallel")),
        name="na_attn",
    )(proj, proj, proj, proj, proj, proj, proj, table)


def _pack_w_in(w):
    sl = lambda off, n: w[:, off:off + n]
    cols = [sl(IN_A_Q, A_Q), sl(IN_B_V, B_V), sl(IN_B_OG, B_V), sl(IN_C_Q, C_W), sl(IN_C_K, C_W), sl(IN_C_V, C_W),
            sl(IN_B_Q, B_QK), sl(IN_B_K, B_QK), sl(IN_A_K, A_KV), sl(IN_A_V, A_KV)]
    wp = jnp.concatenate(cols, axis=1).astype(BF16)
    wgr = jnp.pad(sl(IN_B_GR, 2 * B_GATE_RANK), ((0, 0), (0, GR_PAD - 2 * B_GATE_RANK))).astype(BF16)
    wgate = sl(IN_GL, 3 * D_MODEL).astype(BF16)
    return wp, wgr, wgate


def _pack_gla_w2(w2):
    z = jnp.zeros((GR_PAD, B_QK), F32)
    w2f = z.at[0:B_GATE_RANK].set(w2[0]).astype(BF16)
    w2b = z.at[B_GATE_RANK:2 * B_GATE_RANK].set(w2[1]).astype(BF16)
    return w2f, w2b


def kernel(x_prompt, x_sample, norm1, w_in, sink_a, gla_w2, gla_b, gla_norm, rpb_c, w_br_a, w_br_b, w_br_c, w_out,
           norm2, w_ffn_in, w_ffn_out, norm_f):
    bp, sp, d = x_prompt.shape
    bs, ss, _ = x_sample.shape
    tp, ts = bp * sp, bs * ss
    assert sp % TILE == 0 and ss % TILE == 0 and d == D_MODEL
    seqs = (tp, sp, ss)
    x = jnp.concatenate([x_prompt.reshape(tp, d), x_sample.reshape(ts, d)], axis=0)

    for l in range(DEPTH):
        wp, wgr, wgate = _pack_w_in(w_in[l])
        w2f, w2b = _pack_gla_w2(gla_w2[l])
        table = _na_bias_table(rpb_c[l])
        proj, gr, h = _inproj(x, norm1[l].reshape(1, d), wp, wgr)
        o_a = _win_attn(proj, sink_a[l], seqs)
        o_f, o_bw = _gla(proj, gr, w2f, w2b, gla_b[l], seqs)
        o_c = _na_attn(proj, table, seqs)
        merged = _merge(o_a, o_c, o_f, o_bw, proj, h, gla_norm[l].reshape(1, B_DV), wgate,
                        w_br_a[l].astype(BF16), w_br_b[l].astype(BF16), w_br_c[l].astype(BF16))
        x, h2 = _out_proj(merged, w_out[l].astype(BF16), x, norm2[l].reshape(1, d))
        act = _ffn_in(h2, w_ffn_in[l].astype(BF16))
        x = _mm_res(act, w_ffn_out[l].astype(BF16), x, 1024, 512, "ffn_out")

    g = norm_f.reshape(1, d)
    y_p = _final_norm(x, g, 0, tp).reshape(bp, sp, d)
    y_s = _final_norm(x, g, tp, ts).reshape(bs, ss, d)
    return (y_p, y_s)
```

```python
import functools

import numpy as np
import jax
import jax.numpy as jnp
from jax import lax
from jax.experimental import pallas as pl
from jax.experimental.pallas import tpu as pltpu

F32 = jnp.float32
BF16 = jnp.bfloat16

D_MODEL = 2048
DEPTH = 2
GRID_W = 64
HEAD_DIM = 128
A_HEADS = 8
A_KV_HEADS = 2
A_GROUP = A_HEADS // A_KV_HEADS
A_WINDOW = 128
A_BLOCK = 128
B_HEADS = 4
B_DK = 128
B_DV = 256
B_CHUNK = 64
B_GATE_RANK = 16
B_GATE_TAU = 16.0
C_HEADS = 8
C_WIN_R = 8
C_WIN_C = 16
D_FF = -(-8 * D_MODEL // (3 * 256)) * 256
RMS_EPS = 1e-6
NEG_INF = -1e30
LOG2E = 1.4426950408889634

A_Q = A_HEADS * HEAD_DIM
A_KV = A_KV_HEADS * HEAD_DIM
B_QK = B_HEADS * B_DK
B_V = B_HEADS * B_DV
C_W = C_HEADS * HEAD_DIM
SPLIT_SIZES = (A_Q, A_KV, A_KV, B_QK, B_QK, B_V, B_V, 2 * B_GATE_RANK, C_W, C_W, C_W, 3 * D_MODEL)
SPLIT_OFF = tuple(int(i) for i in np.cumsum((0,) + SPLIT_SIZES))
(IN_A_Q, IN_A_K, IN_A_V, IN_B_Q, IN_B_K, IN_B_V, IN_B_OG, IN_B_GR, IN_C_Q, IN_C_K, IN_C_V, IN_GL) = SPLIT_OFF[:-1]

P_A_Q = 0
P_B_V = 1024
P_B_OG = 2048
P_C_Q = 3072
P_C_K = 4096
P_C_V = 5120
P_B_Q = 6144
P_B_K = 6656
P_A_K = 7168
P_A_V = 7424
P_COLS = 7680
GR_PAD = 128

TILE = 2048
NA_Q_ROWS = 4
NA_KEY_ROWS = 12
NA_Q = NA_Q_ROWS * GRID_W
NA_KEYS = NA_KEY_ROWS * GRID_W
NA_HALO = 256
NA_UNROLL = 4

VMEM_LIMIT = 56 * 1024 * 1024


def _cparams(sem):
    return pltpu.CompilerParams(dimension_semantics=sem, vmem_limit_bytes=VMEM_LIMIT)


def _seq_bounds(pos, seqs):
    tp, sp, ss = seqs
    in_p = pos < tp
    lo_p = lax.div(pos, sp) * sp
    lo_s = tp + lax.div(jnp.maximum(pos - tp, 0), ss) * ss
    lo = jnp.where(in_p, lo_p, lo_s)
    hi = lo + jnp.where(in_p, sp, ss)
    return lo, hi


NORM_ROWS = 256


def _rmsnorm_rows(x_ref, g_ref, h_ref):
    def body(c, carry):
        r = pl.multiple_of(c * NORM_ROWS, NORM_ROWS)
        xs = x_ref[pl.ds(r, NORM_ROWS), :]
        ms = jnp.mean(xs * xs, axis=-1, keepdims=True)
        h_ref[pl.ds(r, NORM_ROWS), :] = (xs * lax.rsqrt(ms + RMS_EPS) * g_ref[...]).astype(h_ref.dtype)
        return carry
    lax.fori_loop(0, x_ref.shape[0] // NORM_ROWS, body, 0)


def _carry_rows(prev, n_in):
    if prev is None:
        return [], [], {}
    prev = list(prev)
    return prev, [pl.BlockSpec(memory_space=pl.ANY)] * len(prev), {n_in + k: k for k in range(len(prev))}


def _inproj_kernel(x_ref, g_ref, w_ref, wgr_ref, *rest):
    proj_ref, gr_ref, h_ref = rest[-3:]

    @pl.when(pl.program_id(1) == 0)
    def _():
        _rmsnorm_rows(x_ref, g_ref, h_ref)
        gr_ref[...] = jnp.dot(h_ref[...], wgr_ref[...], preferred_element_type=F32)
    proj_ref[...] = jnp.dot(h_ref[...], w_ref[...], preferred_element_type=F32).astype(proj_ref.dtype)


def _inproj(x, g, w, wgr, t_total, row0=0, prev=None, bm=1024, bn=1536):
    t, d = x.shape
    n = w.shape[1]
    blk0 = row0 // bm
    extra, extra_specs, aliases = _carry_rows(prev, 4)
    return pl.pallas_call(
        _inproj_kernel,
        grid=(t // bm, n // bn),
        in_specs=[
            pl.BlockSpec((bm, d), lambda i, j: (i, 0)),
            pl.BlockSpec((1, d), lambda i, j: (0, 0)),
            pl.BlockSpec((d, bn), lambda i, j: (0, j)),
            pl.BlockSpec((d, GR_PAD), lambda i, j: (0, 0)),
        ] + extra_specs,
        out_specs=[
            pl.BlockSpec((bm, bn), lambda i, j: (blk0 + i, j)),
            pl.BlockSpec((bm, GR_PAD), lambda i, j: (blk0 + i, 0)),
            pl.BlockSpec((bm, d), lambda i, j: (blk0 + i, 0)),
        ],
        out_shape=[jax.ShapeDtypeStruct((t_total, n), BF16), jax.ShapeDtypeStruct((t_total, GR_PAD), F32),
                   jax.ShapeDtypeStruct((t_total, d), BF16)],
        input_output_aliases=aliases,
        compiler_params=_cparams(("parallel", "arbitrary")),
        name="inproj",
    )(x, g, w, wgr, *extra)


def _ffn_in_kernel(h_ref, wg_ref, wu_ref, act_ref):
    h = h_ref[...]
    gate = jnp.dot(h, wg_ref[...], preferred_element_type=F32)
    up = jnp.dot(h, wu_ref[...], preferred_element_type=F32)
    act_ref[...] = (gate * jax.nn.sigmoid(gate) * up).astype(act_ref.dtype)


def _ffn_in(h, w, bm=1024, bn=512):
    t, d = h.shape
    ff = w.shape[1] // 2
    nj = ff // bn
    return pl.pallas_call(
        _ffn_in_kernel,
        grid=(t // bm, nj),
        in_specs=[
            pl.BlockSpec((bm, d), lambda i, j: (i, 0)),
            pl.BlockSpec((d, bn), lambda i, j: (0, j)),
            pl.BlockSpec((d, bn), lambda i, j: (0, nj + j)),
        ],
        out_specs=pl.BlockSpec((bm, bn), lambda i, j: (i, j)),
        out_shape=jax.ShapeDtypeStruct((t, ff), BF16),
        compiler_params=_cparams(("parallel", "arbitrary")),
        name="ffn_in",
    )(h, w, w)


def _out_proj_kernel(a_ref, w_ref, x_ref, g_ref, *rest):
    xo_ref, h_ref = rest[-2:]
    xo_ref[...] = x_ref[...] + jnp.dot(a_ref[...], w_ref[...], preferred_element_type=F32)
    _rmsnorm_rows(xo_ref, g_ref, h_ref)


def _out_proj(a, w, x, g, row0=0, prev=None, bm=512):
    t_total, k = a.shape
    t, d = x.shape
    blk0 = row0 // bm
    extra, extra_specs, aliases = _carry_rows(prev, 4)
    return pl.pallas_call(
        _out_proj_kernel,
        grid=(t // bm,),
        in_specs=[
            pl.BlockSpec((bm, k), lambda i: (blk0 + i, 0)),
            pl.BlockSpec((k, d), lambda i: (0, 0)),
            pl.BlockSpec((bm, d), lambda i: (i, 0)),
            pl.BlockSpec((1, d), lambda i: (0, 0)),
        ] + extra_specs,
        out_specs=[pl.BlockSpec((bm, d), lambda i: (blk0 + i, 0)), pl.BlockSpec((bm, d), lambda i: (blk0 + i, 0))],
        out_shape=[jax.ShapeDtypeStruct((t_total, d), F32), jax.ShapeDtypeStruct((t_total, d), BF16)],
        input_output_aliases=aliases,
        compiler_params=_cparams(("parallel",)),
        name="out_proj",
    )(a, w, x, g, *extra)


def _mm_res_kernel(a_ref, w_ref, x_ref, o_ref):
    o_ref[...] = x_ref[...] + jnp.dot(a_ref[...], w_ref[...], preferred_element_type=F32)


def _mm_res(a, w, x, bm, bn, name):
    t, k = a.shape
    n = w.shape[1]
    return pl.pallas_call(
        _mm_res_kernel,
        grid=(t // bm, n // bn),
        in_specs=[
            pl.BlockSpec((bm, k), lambda i, j: (i, 0)),
            pl.BlockSpec((k, bn), lambda i, j: (0, j)),
            pl.BlockSpec((bm, bn), lambda i, j: (i, j)),
        ],
        out_specs=pl.BlockSpec((bm, bn), lambda i, j: (i, j)),
        out_shape=jax.ShapeDtypeStruct((t, n), F32),
        compiler_params=_cparams(("parallel", "arbitrary")),
        name=name,
    )(a, w, x)


def _ffn_out_norm_kernel(a_ref, w_ref, x_ref, g_ref, y_ref):
    j = pl.program_id(1)
    bn = w_ref.shape[1]
    c0 = pl.multiple_of(j * bn, bn)
    y_ref[:, pl.ds(c0, bn)] = x_ref[...] + jnp.dot(a_ref[...], w_ref[...], preferred_element_type=F32)

    @pl.when(j == pl.num_programs(1) - 1)
    def _():
        _rmsnorm_rows(y_ref, g_ref, y_ref)


def _ffn_out_norm(a, w, x, g, row0, rows, bm=512, bn=512):
    k = a.shape[1]
    d = w.shape[1]
    blk0 = row0 // bm
    return pl.pallas_call(
        _ffn_out_norm_kernel,
        grid=(rows // bm, d // bn),
        in_specs=[
            pl.BlockSpec((bm, k), lambda i, j: (blk0 + i, 0)),
            pl.BlockSpec((k, bn), lambda i, j: (0, j)),
            pl.BlockSpec((bm, bn), lambda i, j: (blk0 + i, j)),
            pl.BlockSpec((1, d), lambda i, j: (0, 0)),
        ],
        out_specs=pl.BlockSpec((bm, d), lambda i, j: (i, 0)),
        out_shape=jax.ShapeDtypeStruct((rows, d), F32),
        compiler_params=_cparams(("parallel", "arbitrary")),
        name="ffn_out_norm",
    )(a, w, x, g)


MERGE_ROWS = 256


def _merge_kernel(oa_ref, oc_ref, of_ref, obw_ref, og_ref, gain_ref, h_ref, wga_ref, wgb_ref, wgc_ref,
                  wa_ref, wb_ref, wc_ref, out_ref, ob_ref):
    @pl.when(pl.program_id(1) == 0)
    def _():
        def body(c, carry):
            r = pl.multiple_of(c * MERGE_ROWS, MERGE_ROWS)
            o = of_ref[pl.ds(r, MERGE_ROWS), :].astype(F32) + obw_ref[pl.ds(r, MERGE_ROWS), :].astype(F32)
            og = og_ref[pl.ds(r, MERGE_ROWS), :].astype(F32)
            for h in range(B_HEADS):
                sl = slice(h * B_DV, (h + 1) * B_DV)
                oh = o[:, sl]
                ms = jnp.mean(oh * oh, axis=-1, keepdims=True)
                ogh = og[:, sl]
                y = oh * lax.rsqrt(ms + RMS_EPS) * gain_ref[...] * (ogh * jax.nn.sigmoid(ogh))
                ob_ref[pl.ds(r, MERGE_ROWS), sl] = y.astype(ob_ref.dtype)
            return carry
        lax.fori_loop(0, of_ref.shape[0] // MERGE_ROWS, body, 0)

    h = h_ref[...]
    ga = jax.nn.sigmoid(jnp.dot(h, wga_ref[...], preferred_element_type=F32))
    acc = ga * jnp.dot(oa_ref[...], wa_ref[...], preferred_element_type=F32)
    gb = jax.nn.sigmoid(jnp.dot(h, wgb_ref[...], preferred_element_type=F32))
    acc = acc + gb * jnp.dot(ob_ref[...], wb_ref[...], preferred_element_type=F32)
    gc = jax.nn.sigmoid(jnp.dot(h, wgc_ref[...], preferred_element_type=F32))
    acc = acc + gc * jnp.dot(oc_ref[...], wc_ref[...], preferred_element_type=F32)
    out_ref[...] = acc.astype(out_ref.dtype)


def _merge(o_a, o_c, o_f, o_bw, proj, h, gain, wgate, wa, wb, wc, bm=1024, bn=256):
    t = o_a.shape[0]
    d = wa.shape[1]
    kb = o_a.shape[1]
    gstep = d // bn
    wide = lambda i, j: (i, 0)
    return pl.pallas_call(
        _merge_kernel,
        grid=(t // bm, d // bn),
        in_specs=[
            pl.BlockSpec((bm, kb), wide),
            pl.BlockSpec((bm, kb), wide),
            pl.BlockSpec((bm, kb), wide),
            pl.BlockSpec((bm, kb), wide),
            pl.BlockSpec((bm, B_V), lambda i, j: (i, P_B_OG // B_V)),
            pl.BlockSpec((1, B_DV), lambda i, j: (0, 0)),
            pl.BlockSpec((bm, d), wide),
            pl.BlockSpec((d, bn), lambda i, j: (0, j)),
            pl.BlockSpec((d, bn), lambda i, j: (0, gstep + j)),
            pl.BlockSpec((d, bn), lambda i, j: (0, 2 * gstep + j)),
            pl.BlockSpec((kb, bn), lambda i, j: (0, j)),
            pl.BlockSpec((kb, bn), lambda i, j: (0, j)),
            pl.BlockSpec((kb, bn), lambda i, j: (0, j)),
        ],
        out_specs=pl.BlockSpec((bm, bn), lambda i, j: (i, j)),
        out_shape=jax.ShapeDtypeStruct((t, d), BF16),
        scratch_shapes=[pltpu.VMEM((bm, kb), BF16)],
        compiler_params=_cparams(("parallel", "arbitrary")),
        name="merge",
    )(o_a, o_c, o_f, o_bw, proj, gain, h, wgate, wgate, wgate, wa, wb, wc)


A_KEYS = 3 * A_BLOCK
A_ROWS = A_GROUP * A_BLOCK


def _nt_dot(a, b):
    return lax.dot_general(a, b, (((1,), (1,)), ((), ())), preferred_element_type=F32)


def _win_kernel(seqs, sink_ref, q_ref, kp_ref, kc_ref, kn_ref, vp_ref, vc_ref, vn_ref, o_ref, kbuf, vbuf):
    tile_start = pl.program_id(0) * TILE
    seq_lo, seq_hi = _seq_bounds(tile_start, seqs)
    kbuf[0:A_BLOCK, :] = kp_ref[...]
    kbuf[A_BLOCK:A_BLOCK + TILE, :] = kc_ref[...]
    kbuf[A_BLOCK + TILE:, :] = kn_ref[...]
    vbuf[0:A_BLOCK, :] = vp_ref[...]
    vbuf[A_BLOCK:A_BLOCK + TILE, :] = vc_ref[...]
    vbuf[A_BLOCK + TILE:, :] = vn_ref[...]

    row = lax.broadcasted_iota(jnp.int32, (A_ROWS, A_KEYS), 0)
    col = lax.broadcasted_iota(jnp.int32, (A_ROWS, A_KEYS), 1)
    dist = jnp.abs((row % A_BLOCK) - (col - A_BLOCK))
    in_window = dist <= A_WINDOW
    distf = dist.astype(F32)
    grp = lax.broadcasted_iota(jnp.int32, (A_ROWS, 1), 0) // A_BLOCK
    kcol = lax.broadcasted_iota(jnp.int32, (1, A_KEYS), 1)
    qk_scale = HEAD_DIM ** -0.5 * LOG2E

    nbias, sink = [], []
    for kvh in range(A_KV_HEADS):
        slope = jnp.zeros((A_ROWS, 1), F32)
        snk = jnp.zeros((A_ROWS, 1), F32)
        for g in range(A_GROUP):
            h = kvh * A_GROUP + g
            slope = jnp.where(grp == g, 2.0 ** (-8.0 * (h + 1) / A_HEADS) * LOG2E, slope)
            snk = jnp.where(grp == g, sink_ref[h] * LOG2E, snk)
        nbias.append(jnp.where(in_window, -slope * distf, NEG_INF))
        sink.append(snk)

    def body(n, carry):
        r = pl.multiple_of(n * A_BLOCK, A_BLOCK)
        kabs = tile_start - A_BLOCK + r + kcol
        kbias = jnp.where((kabs >= seq_lo) & (kabs < seq_hi), 0.0, NEG_INF)
        t = []
        for kvh in range(A_KV_HEADS):
            q4 = jnp.concatenate(
                [q_ref[pl.ds(r, A_BLOCK), (kvh * A_GROUP + g) * HEAD_DIM:(kvh * A_GROUP + g + 1) * HEAD_DIM]
                 for g in range(A_GROUP)], axis=0)
            kw = kbuf[pl.ds(r, A_KEYS), kvh * HEAD_DIM:(kvh + 1) * HEAD_DIM]
            t.append(_nt_dot(q4, kw) * qk_scale + nbias[kvh] + kbias)
        p, den = [], []
        for kvh in range(A_KV_HEADS):
            m = jnp.maximum(jnp.max(t[kvh], axis=-1, keepdims=True), sink[kvh])
            e = jnp.exp2(t[kvh] - m)
            den.append(jnp.sum(e, axis=-1, keepdims=True) + jnp.exp2(sink[kvh] - m))
            p.append(e.astype(BF16))
        for kvh in range(A_KV_HEADS):
            vw = vbuf[pl.ds(r, A_KEYS), kvh * HEAD_DIM:(kvh + 1) * HEAD_DIM]
            o = jnp.dot(p[kvh], vw, preferred_element_type=F32) / den[kvh]
            for g in range(A_GROUP):
                h = kvh * A_GROUP + g
                o_ref[pl.ds(r, A_BLOCK), h * HEAD_DIM:(h + 1) * HEAD_DIM] = (
                    o[g * A_BLOCK:(g + 1) * A_BLOCK].astype(o_ref.dtype))
        return carry
    lax.fori_loop(0, TILE // A_BLOCK, body, 0)


def _win_attn(proj, sink, seqs):
    t = proj.shape[0]
    nt = t // TILE
    per = TILE // A_BLOCK
    last = t // A_BLOCK - 1
    kblk = P_A_K // A_KV
    vblk = P_A_V // A_KV
    prev = lambda col: (lambda g: (jnp.maximum(g * per - 1, 0), col))
    cur = lambda col: (lambda g: (g, col))
    nxt = lambda col: (lambda g: (jnp.minimum((g + 1) * per, last), col))
    return pl.pallas_call(
        functools.partial(_win_kernel, seqs),
        grid=(nt,),
        in_specs=[
            pl.BlockSpec(memory_space=pltpu.SMEM),
            pl.BlockSpec((TILE, A_Q), lambda g: (g, P_A_Q // A_Q)),
            pl.BlockSpec((A_BLOCK, A_KV), prev(kblk)),
            pl.BlockSpec((TILE, A_KV), cur(kblk)),
            pl.BlockSpec((A_BLOCK, A_KV), nxt(kblk)),
            pl.BlockSpec((A_BLOCK, A_KV), prev(vblk)),
            pl.BlockSpec((TILE, A_KV), cur(vblk)),
            pl.BlockSpec((A_BLOCK, A_KV), nxt(vblk)),
        ],
        out_specs=pl.BlockSpec((TILE, A_Q), lambda g: (g, 0)),
        out_shape=jax.ShapeDtypeStruct((t, A_Q), BF16),
        scratch_shapes=[pltpu.VMEM((TILE + 2 * A_BLOCK, A_KV), BF16),
                        pltpu.VMEM((TILE + 2 * A_BLOCK, A_KV), BF16)],
        compiler_params=_cparams(("parallel",)),
        name="win_attn",
    )(sink, proj, proj, proj, proj, proj, proj, proj)


GLA_TILE = 1024
GLA_BLK = 256


def _tn_dot(a, b):
    return lax.dot_general(a, b, (((0,), (0,)), ((), ())), preferred_element_type=F32)


def _gla_prepare(dirs, bias_ref):
    row = lax.broadcasted_iota(jnp.int32, (GLA_BLK, GLA_BLK), 0)
    col = lax.broadcasted_iota(jnp.int32, (GLA_BLK, GLA_BLK), 1)
    same = (row // B_CHUNK) == (col // B_CHUNK)
    tris = [same & ((row >= col) if d[-1] else (row <= col)) for d in dirs]
    tris_bf = [t.astype(BF16) for t in tris]
    biases = [bias_ref[n:n + 1, :] for n in range(len(dirs))]

    def body(i, carry):
        r = pl.multiple_of(i * GLA_BLK, GLA_BLK)
        rows = pl.ds(r, GLA_BLK)
        his, los = [], []
        for (q_ref, k_ref, v_ref, gr_ref, w2_ref, qt_ref, kd_ref, dec_ref, oi_ref, _), bias in zip(dirs, biases):
            y = (jnp.dot(gr_ref[rows, :].astype(BF16), w2_ref[...], preferred_element_type=F32) + bias) * LOG2E
            lg = (jnp.minimum(y, 0.0) - jnp.log2(1.0 + jnp.exp2(-jnp.abs(y)))) * (1.0 / B_GATE_TAU)
            hi = lg.astype(BF16)
            his.append(hi)
            los.append((lg - hi.astype(F32)).astype(BF16))
        bs, tots = [], []
        for n in range(len(dirs)):
            b = (jnp.dot(tris_bf[n], his[n], preferred_element_type=F32)
                 + jnp.dot(tris_bf[n], los[n], preferred_element_type=F32))
            bs.append(b)
            bc = b.reshape(GLA_BLK // B_CHUNK, B_CHUNK, B_QK)
            tots.append(jnp.broadcast_to(jnp.min(bc, axis=1, keepdims=True), bc.shape).reshape(GLA_BLK, B_QK))
        qts, kts = [], []
        for n, (q_ref, k_ref, v_ref, gr_ref, w2_ref, qt_ref, kd_ref, dec_ref, oi_ref, _) in enumerate(dirs):
            b, tot = bs[n], tots[n]
            q = q_ref[rows, :].astype(F32) * (B_DK ** -0.5)
            k = k_ref[rows, :].astype(F32)
            qt = (q * jnp.exp2(b)).astype(BF16)
            qts.append(qt)
            kts.append((k * jnp.exp2(-b)).astype(BF16))
            qt_ref[rows, :] = qt
            kd_ref[rows, :] = (k * jnp.exp2(tot - b)).astype(BF16)
            dec_ref[rows, :] = jnp.exp2(tot)
        for h in range(B_HEADS):
            ks = slice(h * B_DK, (h + 1) * B_DK)
            vs = slice(h * B_DV, (h + 1) * B_DV)
            for n, d in enumerate(dirs):
                v_ref, oi_ref = d[2], d[8]
                a = jnp.where(tris[n], _nt_dot(qts[n][:, ks], kts[n][:, ks]), 0.0).astype(BF16)
                oi_ref[rows, vs] = jnp.dot(a, v_ref[rows, vs], preferred_element_type=F32)
        return carry
    lax.fori_loop(0, GLA_TILE // GLA_BLK, body, 0)


def _gla_scan_chunks(chunks):
    work = [(c, h) for h in range(B_HEADS) for c in chunks]
    sts = [c[7][h] for c, h in work]
    for (c, h), st in zip(work, sts):
        r, v_ref, qt_ref, kd_ref, dec_ref, oi_ref, o_ref, st_ref = c
        rows = pl.ds(r, B_CHUNK)
        ks = slice(h * B_DK, (h + 1) * B_DK)
        vs = slice(h * B_DV, (h + 1) * B_DV)
        o = oi_ref[rows, vs] + _nt_dot(qt_ref[rows, ks], st.astype(BF16))
        o_ref[rows, vs] = o.astype(o_ref.dtype)
    for (c, h), st in zip(work, sts):
        r, v_ref, qt_ref, kd_ref, dec_ref, oi_ref, o_ref, st_ref = c
        rows = pl.ds(r, B_CHUNK)
        ks = slice(h * B_DK, (h + 1) * B_DK)
        vs = slice(h * B_DV, (h + 1) * B_DV)
        st_ref[h] = st * dec_ref[pl.ds(r, 1), ks] + _tn_dot(v_ref[rows, vs], kd_ref[rows, ks])


def _gla_kernel(seqs, qf_ref, kf_ref, vf_ref, grf_ref, qb_ref, kb_ref, vb_ref, grb_ref, w2f_ref, w2b_ref,
                bias_ref, of_ref, ob_ref, sf_ref, sb_ref, qtf, kdf, decf, oif, qtb, kdb, decb, oib):
    i = pl.program_id(0)
    nt = pl.num_programs(0)
    f_start = i * GLA_TILE
    b_start = (nt - 1 - i) * GLA_TILE
    f_lo, _ = _seq_bounds(f_start, seqs)
    _, b_hi = _seq_bounds(b_start, seqs)

    @pl.when(f_start == f_lo)
    def _():
        sf_ref[...] = jnp.zeros_like(sf_ref)

    @pl.when(b_start + GLA_TILE == b_hi)
    def _():
        sb_ref[...] = jnp.zeros_like(sb_ref)

    _gla_prepare([(qf_ref, kf_ref, vf_ref, grf_ref, w2f_ref, qtf, kdf, decf, oif, True),
                  (qb_ref, kb_ref, vb_ref, grb_ref, w2b_ref, qtb, kdb, decb, oib, False)], bias_ref)

    nchunk = GLA_TILE // B_CHUNK

    def body(c, carry):
        rf = pl.multiple_of(c * B_CHUNK, B_CHUNK)
        rb = pl.multiple_of((nchunk - 1 - c) * B_CHUNK, B_CHUNK)
        _gla_scan_chunks([(rf, vf_ref, qtf, kdf, decf, oif, of_ref, sf_ref),
                          (rb, vb_ref, qtb, kdb, decb, oib, ob_ref, sb_ref)])
        return carry
    lax.fori_loop(0, nchunk, body, 0, unroll=2)


def _gla(proj, gr, w2f, w2b, bias, seqs):
    t = proj.shape[0]
    nt = t // GLA_TILE
    fwd = lambda col: (lambda i: (i, col))
    bwd = lambda col: (lambda i: (nt - 1 - i, col))
    qblk, kblk, vblk = P_B_Q // B_QK, P_B_K // B_QK, P_B_V // B_V

    def specs(mk):
        return [pl.BlockSpec((GLA_TILE, B_QK), mk(qblk)), pl.BlockSpec((GLA_TILE, B_QK), mk(kblk)),
                pl.BlockSpec((GLA_TILE, B_V), mk(vblk)), pl.BlockSpec((GLA_TILE, GR_PAD), mk(0))]

    const = lambda i: (0, 0)
    state = pltpu.VMEM((B_HEADS, B_DV, B_DK), F32)
    per_dir = [pltpu.VMEM((GLA_TILE, B_QK), BF16), pltpu.VMEM((GLA_TILE, B_QK), BF16),
               pltpu.VMEM((GLA_TILE, B_QK), F32), pltpu.VMEM((GLA_TILE, B_V), F32)]
    return pl.pallas_call(
        functools.partial(_gla_kernel, seqs),
        grid=(nt,),
        in_specs=specs(fwd) + specs(bwd) + [
            pl.BlockSpec((GR_PAD, B_QK), const), pl.BlockSpec((GR_PAD, B_QK), const),
            pl.BlockSpec((2, B_QK), const)],
        out_specs=[pl.BlockSpec((GLA_TILE, B_V), fwd(0)), pl.BlockSpec((GLA_TILE, B_V), bwd(0))],
        out_shape=[jax.ShapeDtypeStruct((t, B_V), BF16), jax.ShapeDtypeStruct((t, B_V), BF16)],
        scratch_shapes=[state, state] + per_dir + per_dir,
        compiler_params=_cparams(("arbitrary",)),
        name="gla",
    )(proj, proj, proj, gr, proj, proj, proj, gr, w2f, w2b, bias)


NA_PATTERNS = 3
_NA_ROWS_FOR_TABLE = 32
_NA_PATTERN_ROWS = (0, 2 * NA_Q_ROWS, _NA_ROWS_FOR_TABLE - NA_Q_ROWS)


def _na_bias_table(rpb):
    R = _NA_ROWS_FOR_TABLE
    kr = C_WIN_R
    key_rows = NA_KEY_ROWS
    rpb = rpb.astype(F32) * LOG2E
    n_dr = rpb.shape[1]
    edge = GRID_W - C_WIN_C
    ext = jnp.concatenate([jnp.repeat(rpb[:, :, :1], edge, axis=2), rpb, jnp.repeat(rpb[:, :, -1:], edge + 1, axis=2)], axis=2)
    skew = jnp.tile(ext, (1, 1, GRID_W))[:, :, :GRID_W * (2 * GRID_W - 1)].reshape(C_HEADS, n_dr, GRID_W, 2 * GRID_W - 1)
    col = skew[:, :, :, GRID_W - 1:]
    cq = np.arange(GRID_W)[:, None]
    ck = np.arange(GRID_W)[None, :]
    col_start = np.clip(cq - C_WIN_C // 2, 0, GRID_W - C_WIN_C)
    col_ok = (ck >= col_start) & (ck < col_start + C_WIN_C)
    col = jnp.where(col_ok[None, None], col, NEG_INF)
    pieces = []
    for r in _NA_PATTERN_ROWS:
        start = int(np.clip(r - kr // 2, 0, R - key_rows))
        for jj in range(NA_Q_ROWS):
            qrow = r + jj
            ws = int(np.clip(qrow - kr // 2, 0, R - kr))
            lead, a0 = ws - start, ws - qrow + (C_WIN_R - 1)
            assert 0 <= lead <= key_rows - kr and 0 <= a0 <= n_dr - kr
            pieces.append(jnp.pad(col[:, a0:a0 + kr], ((0, 0), (lead, key_rows - kr - lead), (0, 0), (0, 0)),
                                  constant_values=NEG_INF))
    tab = jnp.stack(pieces, axis=1)
    tab = tab.reshape(C_HEADS, NA_PATTERNS, NA_Q_ROWS, key_rows, GRID_W, GRID_W).transpose(0, 1, 2, 4, 3, 5)
    return tab.reshape(C_HEADS, NA_PATTERNS, NA_Q, NA_KEYS)


def _na_kernel(seqs, q_ref, kp_ref, kc_ref, kn_ref, vp_ref, vc_ref, vn_ref, tab_ref, o_ref, kbuf, vbuf):
    tile_start = pl.program_id(1) * TILE
    seq_lo, seq_hi = _seq_bounds(tile_start, seqs)
    r0 = lax.div(tile_start, GRID_W)
    row_lo = lax.div(seq_lo, GRID_W)
    row_hi = lax.div(seq_hi, GRID_W)
    kbuf[0:NA_HALO, :] = kp_ref[...]
    kbuf[NA_HALO:NA_HALO + TILE, :] = kc_ref[...]
    kbuf[NA_HALO + TILE:, :] = kn_ref[...]
    vbuf[0:NA_HALO, :] = vp_ref[...]
    vbuf[NA_HALO:NA_HALO + TILE, :] = vc_ref[...]
    vbuf[NA_HALO + TILE:, :] = vn_ref[...]
    qk_scale = HEAD_DIM ** -0.5 * LOG2E

    def body(it, carry):
        qs, off, pid = [], [], []
        for u in range(NA_UNROLL):
            grp = it * NA_UNROLL + u
            r = r0 + NA_Q_ROWS * grp
            start = jnp.clip(r - C_WIN_R // 2, row_lo, row_hi - NA_KEY_ROWS)
            off.append(pl.multiple_of((start - r0 + NA_HALO // GRID_W) * GRID_W, NA_Q))
            pid.append(jnp.where(r == row_lo, 0, jnp.where(r == row_hi - NA_Q_ROWS, 2, 1)))
            qs.append(pl.multiple_of(grp * NA_Q, NA_Q))
        t = [_nt_dot(q_ref[pl.ds(qs[u], NA_Q), :], kbuf[pl.ds(off[u], NA_KEYS), :]) * qk_scale + tab_ref[0, pid[u]]
             for u in range(NA_UNROLL)]
        p, den = [], []
        for u in range(NA_UNROLL):
            e = jnp.exp2(t[u] - jnp.max(t[u], axis=-1, keepdims=True))
            den.append(jnp.sum(e, axis=-1, keepdims=True))
            p.append(e.astype(BF16))
        for u in range(NA_UNROLL):
            o = jnp.dot(p[u], vbuf[pl.ds(off[u], NA_KEYS), :], preferred_element_type=F32) / den[u]
            o_ref[pl.ds(qs[u], NA_Q), :] = o.astype(o_ref.dtype)
        return carry
    lax.fori_loop(0, TILE // (NA_Q * NA_UNROLL), body, 0)


def _na_attn(proj, table, seqs):
    t = proj.shape[0]
    nt = t // TILE
    per = TILE // NA_HALO
    last = t // NA_HALO - 1
    qblk, kblk, vblk = P_C_Q // HEAD_DIM, P_C_K // HEAD_DIM, P_C_V // HEAD_DIM
    prev = lambda c0: (lambda h, g: (jnp.maximum(g * per - 1, 0), c0 + h))
    cur = lambda c0: (lambda h, g: (g, c0 + h))
    nxt = lambda c0: (lambda h, g: (jnp.minimum((g + 1) * per, last), c0 + h))
    return pl.pallas_call(
        functools.partial(_na_kernel, seqs),
        grid=(C_HEADS, nt),
        in_specs=[
            pl.BlockSpec((TILE, HEAD_DIM), cur(qblk)),
            pl.BlockSpec((NA_HALO, HEAD_DIM), prev(kblk)),
            pl.BlockSpec((TILE, HEAD_DIM), cur(kblk)),
            pl.BlockSpec((NA_HALO, HEAD_DIM), nxt(kblk)),
            pl.BlockSpec((NA_HALO, HEAD_DIM), prev(vblk)),
            pl.BlockSpec((TILE, HEAD_DIM), cur(vblk)),
            pl.BlockSpec((NA_HALO, HEAD_DIM), nxt(vblk)),
            pl.BlockSpec((1, NA_PATTERNS, NA_Q, NA_KEYS), lambda h, g: (h, 0, 0, 0)),
        ],
        out_specs=pl.BlockSpec((TILE, HEAD_DIM), lambda h, g: (g, h)),
        out_shape=jax.ShapeDtypeStruct((t, C_W), BF16),
        scratch_shapes=[pltpu.VMEM((TILE + 2 * NA_HALO, HEAD_DIM), BF16),
                        pltpu.VMEM((TILE + 2 * NA_HALO, HEAD_DIM), BF16)],
        compiler_params=_cparams(("parallel", "parallel")),
        name="na_attn",
    )(proj, proj, proj, proj, proj, proj, proj, table)


def _pack_w_in(w):
    sl = lambda off, n: w[:, off:off + n]
    cols = [sl(IN_A_Q, A_Q), sl(IN_B_V, B_V), sl(IN_B_OG, B_V), sl(IN_C_Q, C_W), sl(IN_C_K, C_W), sl(IN_C_V, C_W),
            sl(IN_B_Q, B_QK), sl(IN_B_K, B_QK), sl(IN_A_K, A_KV), sl(IN_A_V, A_KV)]
    wp = jnp.concatenate(cols, axis=1).astype(BF16)
    wgr = jnp.pad(sl(IN_B_GR, 2 * B_GATE_RANK), ((0, 0), (0, GR_PAD - 2 * B_GATE_RANK))).astype(BF16)
    wgate = sl(IN_GL, 3 * D_MODEL).astype(BF16)
    return wp, wgr, wgate


def _pack_gla_w2(w2):
    z = jnp.zeros((GR_PAD, B_QK), F32)
    w2f = z.at[0:B_GATE_RANK].set(w2[0]).astype(BF16)
    w2b = z.at[B_GATE_RANK:2 * B_GATE_RANK].set(w2[1]).astype(BF16)
    return w2f, w2b


def kernel(x_prompt, x_sample, norm1, w_in, sink_a, gla_w2, gla_b, gla_norm, rpb_c, w_br_a, w_br_b, w_br_c, w_out,
           norm2, w_ffn_in, w_ffn_out, norm_f):
    bp, sp, d = x_prompt.shape
    bs, ss, _ = x_sample.shape
    tp, ts = bp * sp, bs * ss
    assert sp % TILE == 0 and ss % TILE == 0 and d == D_MODEL
    seqs = (tp, sp, ss)
    t = tp + ts
    groups = ((x_prompt.reshape(tp, d), 0), (x_sample.reshape(ts, d), tp))
    x = None

    for l in range(DEPTH):
        wp, wgr, wgate = _pack_w_in(w_in[l])
        w2f, w2b = _pack_gla_w2(gla_w2[l])
        table = _na_bias_table(rpb_c[l])
        g1, g2 = norm1[l].reshape(1, d), norm2[l].reshape(1, d)
        if x is None:
            outs = None
            for xg, row0 in groups:
                outs = _inproj(xg, g1, wp, wgr, t, row0, outs)
            proj, gr, h = outs
        else:
            proj, gr, h = _inproj(x, g1, wp, wgr, t)
        o_a = _win_attn(proj, sink_a[l], seqs)
        o_f, o_bw = _gla(proj, gr, w2f, w2b, gla_b[l], seqs)
        o_c = _na_attn(proj, table, seqs)
        merged = _merge(o_a, o_c, o_f, o_bw, proj, h, gla_norm[l].reshape(1, B_DV), wgate,
                        w_br_a[l].astype(BF16), w_br_b[l].astype(BF16), w_br_c[l].astype(BF16))
        if x is None:
            outs = None
            for xg, row0 in groups:
                outs = _out_proj(merged, w_out[l].astype(BF16), xg, g2, row0, outs)
            x, h2 = outs
        else:
            x, h2 = _out_proj(merged, w_out[l].astype(BF16), x, g2)
        act = _ffn_in(h2, w_ffn_in[l].astype(BF16))
        if l + 1 < DEPTH:
            x = _mm_res(act, w_ffn_out[l].astype(BF16), x, 1024, 512, "ffn_out")

    gf = norm_f.reshape(1, d)
    wfo = w_ffn_out[DEPTH - 1].astype(BF16)
    y_p = _ffn_out_norm(act, wfo, x, gf, 0, tp).reshape(bp, sp, d)
    y_s = _ffn_out_norm(act, wfo, x, gf, tp, ts).reshape(bs, ss, d)
    return (y_p, y_s)
```

```python
import functools

import numpy as np
import jax
import jax.numpy as jnp
from jax import lax
from jax.experimental import pallas as pl
from jax.experimental.pallas import tpu as pltpu

F32 = jnp.float32
BF16 = jnp.bfloat16

D_MODEL = 2048
DEPTH = 2
GRID_W = 64
HEAD_DIM = 128
A_HEADS = 8
A_KV_HEADS = 2
A_GROUP = A_HEADS // A_KV_HEADS
A_WINDOW = 128
A_BLOCK = 128
B_HEADS = 4
B_DK = 128
B_DV = 256
B_CHUNK = 64
B_GATE_RANK = 16
B_GATE_TAU = 16.0
C_HEADS = 8
C_WIN_R = 8
C_WIN_C = 16
D_FF = -(-8 * D_MODEL // (3 * 256)) * 256
RMS_EPS = 1e-6
NEG_INF = -1e30
LOG2E = 1.4426950408889634

A_Q = A_HEADS * HEAD_DIM
A_KV = A_KV_HEADS * HEAD_DIM
B_QK = B_HEADS * B_DK
B_V = B_HEADS * B_DV
C_W = C_HEADS * HEAD_DIM
SPLIT_SIZES = (A_Q, A_KV, A_KV, B_QK, B_QK, B_V, B_V, 2 * B_GATE_RANK, C_W, C_W, C_W, 3 * D_MODEL)
SPLIT_OFF = tuple(int(i) for i in np.cumsum((0,) + SPLIT_SIZES))
(IN_A_Q, IN_A_K, IN_A_V, IN_B_Q, IN_B_K, IN_B_V, IN_B_OG, IN_B_GR, IN_C_Q, IN_C_K, IN_C_V, IN_GL) = SPLIT_OFF[:-1]

P_A_Q = 0
P_B_V = 1024
P_B_OG = 2048
P_C_Q = 3072
P_C_K = 4096
P_C_V = 5120
P_B_Q = 6144
P_B_K = 6656
P_A_K = 7168
P_A_V = 7424
P_COLS = 7680
GR_PAD = 128

TILE = 2048
NA_Q_ROWS = 4
NA_KEY_ROWS = 12
NA_Q = NA_Q_ROWS * GRID_W
NA_KEYS = NA_KEY_ROWS * GRID_W
NA_HALO = 256
NA_UNROLL = 4

VMEM_LIMIT = 56 * 1024 * 1024


def _cparams(sem):
    return pltpu.CompilerParams(dimension_semantics=sem, vmem_limit_bytes=VMEM_LIMIT)


def _seq_bounds(pos, seqs):
    tp, sp, ss = seqs
    in_p = pos < tp
    lo_p = lax.div(pos, sp) * sp
    lo_s = tp + lax.div(jnp.maximum(pos - tp, 0), ss) * ss
    lo = jnp.where(in_p, lo_p, lo_s)
    hi = lo + jnp.where(in_p, sp, ss)
    return lo, hi


NORM_ROWS = 256


def _rmsnorm_rows(x_ref, g_ref, h_ref):
    def body(c, carry):
        r = pl.multiple_of(c * NORM_ROWS, NORM_ROWS)
        xs = x_ref[pl.ds(r, NORM_ROWS), :]
        ms = jnp.mean(xs * xs, axis=-1, keepdims=True)
        h_ref[pl.ds(r, NORM_ROWS), :] = (xs * lax.rsqrt(ms + RMS_EPS) * g_ref[...]).astype(h_ref.dtype)
        return carry
    lax.fori_loop(0, x_ref.shape[0] // NORM_ROWS, body, 0)


def _carry_rows(prev, n_in):
    if prev is None:
        return [], [], {}
    prev = list(prev)
    return prev, [pl.BlockSpec(memory_space=pl.ANY)] * len(prev), {n_in + k: k for k in range(len(prev))}


def _inproj_kernel(x_ref, g_ref, w_ref, wgr_ref, *rest):
    proj_ref, gr_ref, h_ref = rest[-3:]

    @pl.when(pl.program_id(1) == 0)
    def _():
        _rmsnorm_rows(x_ref, g_ref, h_ref)
        gr_ref[...] = jnp.dot(h_ref[...], wgr_ref[...], preferred_element_type=F32)
    proj_ref[...] = jnp.dot(h_ref[...], w_ref[...], preferred_element_type=F32).astype(proj_ref.dtype)


def _inproj(x, g, w, wgr, t_total, row0=0, prev=None, bm=1024, bn=1536):
    t, d = x.shape
    n = w.shape[1]
    blk0 = row0 // bm
    extra, extra_specs, aliases = _carry_rows(prev, 4)
    return pl.pallas_call(
        _inproj_kernel,
        grid=(t // bm, n // bn),
        in_specs=[
            pl.BlockSpec((bm, d), lambda i, j: (i, 0)),
            pl.BlockSpec((1, d), lambda i, j: (0, 0)),
            pl.BlockSpec((d, bn), lambda i, j: (0, j)),
            pl.BlockSpec((d, GR_PAD), lambda i, j: (0, 0)),
        ] + extra_specs,
        out_specs=[
            pl.BlockSpec((bm, bn), lambda i, j: (blk0 + i, j)),
            pl.BlockSpec((bm, GR_PAD), lambda i, j: (blk0 + i, 0)),
            pl.BlockSpec((bm, d), lambda i, j: (blk0 + i, 0)),
        ],
        out_shape=[jax.ShapeDtypeStruct((t_total, n), BF16), jax.ShapeDtypeStruct((t_total, GR_PAD), F32),
                   jax.ShapeDtypeStruct((t_total, d), BF16)],
        input_output_aliases=aliases,
        compiler_params=_cparams(("parallel", "arbitrary")),
        name="inproj",
    )(x, g, w, wgr, *extra)


def _ffn_in_kernel(h_ref, wg_ref, wu_ref, act_ref):
    h = h_ref[...]
    gate = jnp.dot(h, wg_ref[...], preferred_element_type=F32)
    up = jnp.dot(h, wu_ref[...], preferred_element_type=F32)
    act_ref[...] = (gate * jax.nn.sigmoid(gate) * up).astype(act_ref.dtype)


def _ffn_in(h, w, bm=1024, bn=512):
    t, d = h.shape
    ff = w.shape[1] // 2
    nj = ff // bn
    return pl.pallas_call(
        _ffn_in_kernel,
        grid=(t // bm, nj),
        in_specs=[
            pl.BlockSpec((bm, d), lambda i, j: (i, 0)),
            pl.BlockSpec((d, bn), lambda i, j: (0, j)),
            pl.BlockSpec((d, bn), lambda i, j: (0, nj + j)),
        ],
        out_specs=pl.BlockSpec((bm, bn), lambda i, j: (i, j)),
        out_shape=jax.ShapeDtypeStruct((t, ff), BF16),
        compiler_params=_cparams(("parallel", "arbitrary")),
        name="ffn_in",
    )(h, w, w)


def _out_proj_kernel(a_ref, w_ref, x_ref, g_ref, *rest):
    xo_ref, h_ref = rest[-2:]
    xo_ref[...] = x_ref[...] + jnp.dot(a_ref[...], w_ref[...], preferred_element_type=F32)
    _rmsnorm_rows(xo_ref, g_ref, h_ref)


def _out_proj(a, w, x, g, row0=0, prev=None, bm=512):
    t_total, k = a.shape
    t, d = x.shape
    blk0 = row0 // bm
    extra, extra_specs, aliases = _carry_rows(prev, 4)
    return pl.pallas_call(
        _out_proj_kernel,
        grid=(t // bm,),
        in_specs=[
            pl.BlockSpec((bm, k), lambda i: (blk0 + i, 0)),
            pl.BlockSpec((k, d), lambda i: (0, 0)),
            pl.BlockSpec((bm, d), lambda i: (i, 0)),
            pl.BlockSpec((1, d), lambda i: (0, 0)),
        ] + extra_specs,
        out_specs=[pl.BlockSpec((bm, d), lambda i: (blk0 + i, 0)), pl.BlockSpec((bm, d), lambda i: (blk0 + i, 0))],
        out_shape=[jax.ShapeDtypeStruct((t_total, d), F32), jax.ShapeDtypeStruct((t_total, d), BF16)],
        input_output_aliases=aliases,
        compiler_params=_cparams(("parallel",)),
        name="out_proj",
    )(a, w, x, g, *extra)


def _mm_res_kernel(a_ref, w_ref, x_ref, o_ref):
    o_ref[...] = x_ref[...] + jnp.dot(a_ref[...], w_ref[...], preferred_element_type=F32)


def _mm_res(a, w, x, bm, bn, name):
    t, k = a.shape
    n = w.shape[1]
    return pl.pallas_call(
        _mm_res_kernel,
        grid=(t // bm, n // bn),
        in_specs=[
            pl.BlockSpec((bm, k), lambda i, j: (i, 0)),
            pl.BlockSpec((k, bn), lambda i, j: (0, j)),
            pl.BlockSpec((bm, bn), lambda i, j: (i, j)),
        ],
        out_specs=pl.BlockSpec((bm, bn), lambda i, j: (i, j)),
        out_shape=jax.ShapeDtypeStruct((t, n), F32),
        compiler_params=_cparams(("parallel", "arbitrary")),
        name=name,
    )(a, w, x)


def _ffn_out_norm_kernel(a_ref, w_ref, x_ref, g_ref, y_ref):
    j = pl.program_id(1)
    bn = w_ref.shape[1]
    c0 = pl.multiple_of(j * bn, bn)
    y_ref[:, pl.ds(c0, bn)] = x_ref[...] + jnp.dot(a_ref[...], w_ref[...], preferred_element_type=F32)

    @pl.when(j == pl.num_programs(1) - 1)
    def _():
        _rmsnorm_rows(y_ref, g_ref, y_ref)


def _ffn_out_norm(a, w, x, g, row0, rows, bm=512, bn=512):
    k = a.shape[1]
    d = w.shape[1]
    blk0 = row0 // bm
    return pl.pallas_call(
        _ffn_out_norm_kernel,
        grid=(rows // bm, d // bn),
        in_specs=[
            pl.BlockSpec((bm, k), lambda i, j: (blk0 + i, 0)),
            pl.BlockSpec((k, bn), lambda i, j: (0, j)),
            pl.BlockSpec((bm, bn), lambda i, j: (blk0 + i, j)),
            pl.BlockSpec((1, d), lambda i, j: (0, 0)),
        ],
        out_specs=pl.BlockSpec((bm, d), lambda i, j: (i, 0)),
        out_shape=jax.ShapeDtypeStruct((rows, d), F32),
        compiler_params=_cparams(("parallel", "arbitrary")),
        name="ffn_out_norm",
    )(a, w, x, g)


def _gla_epilogue_rows(rows, of_ref, obw_ref, og_ref, gain_ref, ob_ref):
    o = of_ref[rows, :].astype(F32) + obw_ref[rows, :].astype(F32)
    og = og_ref[rows, :].astype(F32)
    for h in range(B_HEADS):
        sl = slice(h * B_DV, (h + 1) * B_DV)
        oh = o[:, sl]
        ms = jnp.mean(oh * oh, axis=-1, keepdims=True)
        ogh = og[:, sl]
        y = oh * lax.rsqrt(ms + RMS_EPS) * gain_ref[...] * (ogh * jax.nn.sigmoid(ogh))
        ob_ref[rows, sl] = y.astype(ob_ref.dtype)


def _merge_kernel(oa_ref, ob_ref, oc_ref, h_ref, wga_ref, wgb_ref, wgc_ref, wa_ref, wb_ref, wc_ref, out_ref):
    h = h_ref[...]
    ga = jax.nn.sigmoid(jnp.dot(h, wga_ref[...], preferred_element_type=F32))
    acc = ga * jnp.dot(oa_ref[...], wa_ref[...], preferred_element_type=F32)
    gb = jax.nn.sigmoid(jnp.dot(h, wgb_ref[...], preferred_element_type=F32))
    acc = acc + gb * jnp.dot(ob_ref[...], wb_ref[...], preferred_element_type=F32)
    gc = jax.nn.sigmoid(jnp.dot(h, wgc_ref[...], preferred_element_type=F32))
    acc = acc + gc * jnp.dot(oc_ref[...], wc_ref[...], preferred_element_type=F32)
    out_ref[...] = acc.astype(out_ref.dtype)


def _merge(o_a, o_b, o_c, h, wgate, wa, wb, wc, bm=1024, bn=512):
    t = o_a.shape[0]
    d = wa.shape[1]
    kb = o_a.shape[1]
    gstep = d // bn
    wide = lambda i, j: (i, 0)
    return pl.pallas_call(
        _merge_kernel,
        grid=(t // bm, d // bn),
        in_specs=[
            pl.BlockSpec((bm, kb), wide),
            pl.BlockSpec((bm, kb), wide),
            pl.BlockSpec((bm, kb), wide),
            pl.BlockSpec((bm, d), wide),
            pl.BlockSpec((d, bn), lambda i, j: (0, j)),
            pl.BlockSpec((d, bn), lambda i, j: (0, gstep + j)),
            pl.BlockSpec((d, bn), lambda i, j: (0, 2 * gstep + j)),
            pl.BlockSpec((kb, bn), lambda i, j: (0, j)),
            pl.BlockSpec((kb, bn), lambda i, j: (0, j)),
            pl.BlockSpec((kb, bn), lambda i, j: (0, j)),
        ],
        out_specs=pl.BlockSpec((bm, bn), lambda i, j: (i, j)),
        out_shape=jax.ShapeDtypeStruct((t, d), BF16),
        compiler_params=_cparams(("parallel", "arbitrary")),
        name="merge",
    )(o_a, o_b, o_c, h, wgate, wgate, wgate, wa, wb, wc)


A_KEYS = 3 * A_BLOCK
A_ROWS = A_GROUP * A_BLOCK


def _nt_dot(a, b):
    return lax.dot_general(a, b, (((1,), (1,)), ((), ())), preferred_element_type=F32)


def _win_kernel(seqs, sink_ref, q_ref, kp_ref, kc_ref, kn_ref, vp_ref, vc_ref, vn_ref, o_ref, kbuf, vbuf):
    tile_start = pl.program_id(0) * TILE
    seq_lo, seq_hi = _seq_bounds(tile_start, seqs)
    kbuf[0:A_BLOCK, :] = kp_ref[...]
    kbuf[A_BLOCK:A_BLOCK + TILE, :] = kc_ref[...]
    kbuf[A_BLOCK + TILE:, :] = kn_ref[...]
    vbuf[0:A_BLOCK, :] = vp_ref[...]
    vbuf[A_BLOCK:A_BLOCK + TILE, :] = vc_ref[...]
    vbuf[A_BLOCK + TILE:, :] = vn_ref[...]

    row = lax.broadcasted_iota(jnp.int32, (A_ROWS, A_KEYS), 0)
    col = lax.broadcasted_iota(jnp.int32, (A_ROWS, A_KEYS), 1)
    dist = jnp.abs((row % A_BLOCK) - (col - A_BLOCK))
    in_window = dist <= A_WINDOW
    distf = dist.astype(F32)
    grp = lax.broadcasted_iota(jnp.int32, (A_ROWS, 1), 0) // A_BLOCK
    kcol = lax.broadcasted_iota(jnp.int32, (1, A_KEYS), 1)
    qk_scale = HEAD_DIM ** -0.5 * LOG2E

    nbias, sink = [], []
    for kvh in range(A_KV_HEADS):
        slope = jnp.zeros((A_ROWS, 1), F32)
        snk = jnp.zeros((A_ROWS, 1), F32)
        for g in range(A_GROUP):
            h = kvh * A_GROUP + g
            slope = jnp.where(grp == g, 2.0 ** (-8.0 * (h + 1) / A_HEADS) * LOG2E, slope)
            snk = jnp.where(grp == g, sink_ref[h] * LOG2E, snk)
        nbias.append(jnp.where(in_window, -slope * distf, NEG_INF))
        sink.append(snk)

    def body(n, carry):
        r = pl.multiple_of(n * A_BLOCK, A_BLOCK)
        kabs = tile_start - A_BLOCK + r + kcol
        kbias = jnp.where((kabs >= seq_lo) & (kabs < seq_hi), 0.0, NEG_INF)
        t = []
        for kvh in range(A_KV_HEADS):
            q4 = jnp.concatenate(
                [q_ref[pl.ds(r, A_BLOCK), (kvh * A_GROUP + g) * HEAD_DIM:(kvh * A_GROUP + g + 1) * HEAD_DIM]
                 for g in range(A_GROUP)], axis=0)
            kw = kbuf[pl.ds(r, A_KEYS), kvh * HEAD_DIM:(kvh + 1) * HEAD_DIM]
            t.append(_nt_dot(q4, kw) * qk_scale + nbias[kvh] + kbias)
        p, den = [], []
        for kvh in range(A_KV_HEADS):
            m = jnp.maximum(jnp.max(t[kvh], axis=-1, keepdims=True), sink[kvh])
            e = jnp.exp2(t[kvh] - m)
            den.append(jnp.sum(e, axis=-1, keepdims=True) + jnp.exp2(sink[kvh] - m))
            p.append(e.astype(BF16))
        for kvh in range(A_KV_HEADS):
            vw = vbuf[pl.ds(r, A_KEYS), kvh * HEAD_DIM:(kvh + 1) * HEAD_DIM]
            o = jnp.dot(p[kvh], vw, preferred_element_type=F32) / den[kvh]
            for g in range(A_GROUP):
                h = kvh * A_GROUP + g
                o_ref[pl.ds(r, A_BLOCK), h * HEAD_DIM:(h + 1) * HEAD_DIM] = (
                    o[g * A_BLOCK:(g + 1) * A_BLOCK].astype(o_ref.dtype))
        return carry
    lax.fori_loop(0, TILE // A_BLOCK, body, 0)


def _win_attn(proj, sink, seqs):
    t = proj.shape[0]
    nt = t // TILE
    per = TILE // A_BLOCK
    last = t // A_BLOCK - 1
    kblk = P_A_K // A_KV
    vblk = P_A_V // A_KV
    prev = lambda col: (lambda g: (jnp.maximum(g * per - 1, 0), col))
    cur = lambda col: (lambda g: (g, col))
    nxt = lambda col: (lambda g: (jnp.minimum((g + 1) * per, last), col))
    return pl.pallas_call(
        functools.partial(_win_kernel, seqs),
        grid=(nt,),
        in_specs=[
            pl.BlockSpec(memory_space=pltpu.SMEM),
            pl.BlockSpec((TILE, A_Q), lambda g: (g, P_A_Q // A_Q)),
            pl.BlockSpec((A_BLOCK, A_KV), prev(kblk)),
            pl.BlockSpec((TILE, A_KV), cur(kblk)),
            pl.BlockSpec((A_BLOCK, A_KV), nxt(kblk)),
            pl.BlockSpec((A_BLOCK, A_KV), prev(vblk)),
            pl.BlockSpec((TILE, A_KV), cur(vblk)),
            pl.BlockSpec((A_BLOCK, A_KV), nxt(vblk)),
        ],
        out_specs=pl.BlockSpec((TILE, A_Q), lambda g: (g, 0)),
        out_shape=jax.ShapeDtypeStruct((t, A_Q), BF16),
        scratch_shapes=[pltpu.VMEM((TILE + 2 * A_BLOCK, A_KV), BF16),
                        pltpu.VMEM((TILE + 2 * A_BLOCK, A_KV), BF16)],
        compiler_params=_cparams(("parallel",)),
        name="win_attn",
    )(sink, proj, proj, proj, proj, proj, proj, proj)


GLA_TILE = 1024
GLA_BLK = 256


def _tn_dot(a, b):
    return lax.dot_general(a, b, (((0,), (0,)), ((), ())), preferred_element_type=F32)


def _gla_prepare(dirs, bias_ref):
    row = lax.broadcasted_iota(jnp.int32, (GLA_BLK, GLA_BLK), 0)
    col = lax.broadcasted_iota(jnp.int32, (GLA_BLK, GLA_BLK), 1)
    same = (row // B_CHUNK) == (col // B_CHUNK)
    tris = [same & ((row >= col) if d[-1] else (row <= col)) for d in dirs]
    tris_bf = [t.astype(BF16) for t in tris]
    biases = [bias_ref[n:n + 1, :] for n in range(len(dirs))]

    def body(i, carry):
        r = pl.multiple_of(i * GLA_BLK, GLA_BLK)
        rows = pl.ds(r, GLA_BLK)
        his, los = [], []
        for (q_ref, k_ref, v_ref, gr_ref, w2_ref, qt_ref, kd_ref, dec_ref, oi_ref, _), bias in zip(dirs, biases):
            y = (jnp.dot(gr_ref[rows, :].astype(BF16), w2_ref[...], preferred_element_type=F32) + bias) * LOG2E
            lg = (jnp.minimum(y, 0.0) - jnp.log2(1.0 + jnp.exp2(-jnp.abs(y)))) * (1.0 / B_GATE_TAU)
            hi = lg.astype(BF16)
            his.append(hi)
            los.append((lg - hi.astype(F32)).astype(BF16))
        bs, tots = [], []
        for n in range(len(dirs)):
            b = (jnp.dot(tris_bf[n], his[n], preferred_element_type=F32)
                 + jnp.dot(tris_bf[n], los[n], preferred_element_type=F32))
            bs.append(b)
            bc = b.reshape(GLA_BLK // B_CHUNK, B_CHUNK, B_QK)
            tots.append(jnp.broadcast_to(jnp.min(bc, axis=1, keepdims=True), bc.shape).reshape(GLA_BLK, B_QK))
        qts, kts = [], []
        for n, (q_ref, k_ref, v_ref, gr_ref, w2_ref, qt_ref, kd_ref, dec_ref, oi_ref, _) in enumerate(dirs):
            b, tot = bs[n], tots[n]
            q = q_ref[rows, :].astype(F32) * (B_DK ** -0.5)
            k = k_ref[rows, :].astype(F32)
            qt = (q * jnp.exp2(b)).astype(BF16)
            qts.append(qt)
            kts.append((k * jnp.exp2(-b)).astype(BF16))
            qt_ref[rows, :] = qt
            kd_ref[rows, :] = (k * jnp.exp2(tot - b)).astype(BF16)
            dec_ref[rows, :] = jnp.exp2(tot)
        for h in range(B_HEADS):
            ks = slice(h * B_DK, (h + 1) * B_DK)
            vs = slice(h * B_DV, (h + 1) * B_DV)
            for n, d in enumerate(dirs):
                v_ref, oi_ref = d[2], d[8]
                a = jnp.where(tris[n], _nt_dot(qts[n][:, ks], kts[n][:, ks]), 0.0).astype(BF16)
                oi_ref[rows, vs] = jnp.dot(a, v_ref[rows, vs], preferred_element_type=F32)
        return carry
    lax.fori_loop(0, GLA_TILE // GLA_BLK, body, 0)


def _gla_scan_chunks(chunks):
    work = [(c, h) for h in range(B_HEADS) for c in chunks]
    sts = [c[7][h] for c, h in work]
    for (c, h), st in zip(work, sts):
        r, v_ref, qt_ref, kd_ref, dec_ref, oi_ref, o_ref, st_ref = c
        rows = pl.ds(r, B_CHUNK)
        ks = slice(h * B_DK, (h + 1) * B_DK)
        vs = slice(h * B_DV, (h + 1) * B_DV)
        o = oi_ref[rows, vs] + _nt_dot(qt_ref[rows, ks], st.astype(BF16))
        o_ref[rows, vs] = o.astype(o_ref.dtype)
    for (c, h), st in zip(work, sts):
        r, v_ref, qt_ref, kd_ref, dec_ref, oi_ref, o_ref, st_ref = c
        rows = pl.ds(r, B_CHUNK)
        ks = slice(h * B_DK, (h + 1) * B_DK)
        vs = slice(h * B_DV, (h + 1) * B_DV)
        st_ref[h] = st * dec_ref[pl.ds(r, 1), ks] + _tn_dot(v_ref[rows, vs], kd_ref[rows, ks])


def _gla_kernel(seqs, qf_ref, kf_ref, vf_ref, grf_ref, qb_ref, kb_ref, vb_ref, grb_ref, w2f_ref, w2b_ref,
                bias_ref, of_ref, ob_ref, sf_ref, sb_ref, qtf, kdf, decf, oif, qtb, kdb, decb, oib):
    i = pl.program_id(0)
    nt = pl.num_programs(0)
    f_start = i * GLA_TILE
    b_start = (nt - 1 - i) * GLA_TILE
    f_lo, _ = _seq_bounds(f_start, seqs)
    _, b_hi = _seq_bounds(b_start, seqs)

    @pl.when(f_start == f_lo)
    def _():
        sf_ref[...] = jnp.zeros_like(sf_ref)

    @pl.when(b_start + GLA_TILE == b_hi)
    def _():
        sb_ref[...] = jnp.zeros_like(sb_ref)

    _gla_prepare([(qf_ref, kf_ref, vf_ref, grf_ref, w2f_ref, qtf, kdf, decf, oif, True),
                  (qb_ref, kb_ref, vb_ref, grb_ref, w2b_ref, qtb, kdb, decb, oib, False)], bias_ref)

    nchunk = GLA_TILE // B_CHUNK

    def body(c, carry):
        rf = pl.multiple_of(c * B_CHUNK, B_CHUNK)
        rb = pl.multiple_of((nchunk - 1 - c) * B_CHUNK, B_CHUNK)
        _gla_scan_chunks([(rf, vf_ref, qtf, kdf, decf, oif, of_ref, sf_ref),
                          (rb, vb_ref, qtb, kdb, decb, oib, ob_ref, sb_ref)])
        return carry
    lax.fori_loop(0, nchunk, body, 0, unroll=2)


def _gla(proj, gr, w2f, w2b, bias, seqs):
    t = proj.shape[0]
    nt = t // GLA_TILE
    fwd = lambda col: (lambda i: (i, col))
    bwd = lambda col: (lambda i: (nt - 1 - i, col))
    qblk, kblk, vblk = P_B_Q // B_QK, P_B_K // B_QK, P_B_V // B_V

    def specs(mk):
        return [pl.BlockSpec((GLA_TILE, B_QK), mk(qblk)), pl.BlockSpec((GLA_TILE, B_QK), mk(kblk)),
                pl.BlockSpec((GLA_TILE, B_V), mk(vblk)), pl.BlockSpec((GLA_TILE, GR_PAD), mk(0))]

    const = lambda i: (0, 0)
    state = pltpu.VMEM((B_HEADS, B_DV, B_DK), F32)
    per_dir = [pltpu.VMEM((GLA_TILE, B_QK), BF16), pltpu.VMEM((GLA_TILE, B_QK), BF16),
               pltpu.VMEM((GLA_TILE, B_QK), F32), pltpu.VMEM((GLA_TILE, B_V), F32)]
    return pl.pallas_call(
        functools.partial(_gla_kernel, seqs),
        grid=(nt,),
        in_specs=specs(fwd) + specs(bwd) + [
            pl.BlockSpec((GR_PAD, B_QK), const), pl.BlockSpec((GR_PAD, B_QK), const),
            pl.BlockSpec((2, B_QK), const)],
        out_specs=[pl.BlockSpec((GLA_TILE, B_V), fwd(0)), pl.BlockSpec((GLA_TILE, B_V), bwd(0))],
        out_shape=[jax.ShapeDtypeStruct((t, B_V), BF16), jax.ShapeDtypeStruct((t, B_V), BF16)],
        scratch_shapes=[state, state] + per_dir + per_dir,
        compiler_params=_cparams(("arbitrary",)),
        name="gla",
    )(proj, proj, proj, gr, proj, proj, proj, gr, w2f, w2b, bias)


NA_PATTERNS = 3
_NA_ROWS_FOR_TABLE = 32
_NA_PATTERN_ROWS = (0, 2 * NA_Q_ROWS, _NA_ROWS_FOR_TABLE - NA_Q_ROWS)


def _na_bias_table(rpb):
    R = _NA_ROWS_FOR_TABLE
    kr = C_WIN_R
    key_rows = NA_KEY_ROWS
    rpb = rpb.astype(F32) * LOG2E
    n_dr = rpb.shape[1]
    edge = GRID_W - C_WIN_C
    ext = jnp.concatenate([jnp.repeat(rpb[:, :, :1], edge, axis=2), rpb, jnp.repeat(rpb[:, :, -1:], edge + 1, axis=2)], axis=2)
    skew = jnp.tile(ext, (1, 1, GRID_W))[:, :, :GRID_W * (2 * GRID_W - 1)].reshape(C_HEADS, n_dr, GRID_W, 2 * GRID_W - 1)
    col = skew[:, :, :, GRID_W - 1:]
    cq = np.arange(GRID_W)[:, None]
    ck = np.arange(GRID_W)[None, :]
    col_start = np.clip(cq - C_WIN_C // 2, 0, GRID_W - C_WIN_C)
    col_ok = (ck >= col_start) & (ck < col_start + C_WIN_C)
    col = jnp.where(col_ok[None, None], col, NEG_INF)
    colp = jnp.pad(col.transpose(0, 2, 1, 3), ((0, 0), (0, 0), (key_rows - kr, key_rows - kr), (0, 0)))
    pieces, row_ok = [], np.zeros((NA_PATTERNS * NA_Q_ROWS, 1, key_rows, 1), bool)
    for r in _NA_PATTERN_ROWS:
        start = int(np.clip(r - kr // 2, 0, R - key_rows))
        for jj in range(NA_Q_ROWS):
            qrow = r + jj
            ws = int(np.clip(qrow - kr // 2, 0, R - kr))
            lead, a0 = ws - start, ws - qrow + (C_WIN_R - 1)
            assert 0 <= lead <= key_rows - kr and 0 <= a0 <= n_dr - kr
            row_ok[len(pieces), 0, lead:lead + kr, 0] = True
            first = a0 - lead + key_rows - kr
            pieces.append(colp[:, :, first:first + key_rows])
    tab = jnp.where(row_ok[None], jnp.stack(pieces, axis=1), NEG_INF)
    return tab.reshape(C_HEADS, NA_PATTERNS, NA_Q, NA_KEYS)


def _na_kernel(seqs, q_ref, kp_ref, kc_ref, kn_ref, vp_ref, vc_ref, vn_ref, tab_ref, of_ref, obw_ref, og_ref,
               gain_ref, o_ref, ob_ref, kbuf, vbuf):
    tile_start = pl.program_id(1) * TILE
    seq_lo, seq_hi = _seq_bounds(tile_start, seqs)
    r0 = lax.div(tile_start, GRID_W)
    row_lo = lax.div(seq_lo, GRID_W)
    row_hi = lax.div(seq_hi, GRID_W)
    kbuf[0:NA_HALO, :] = kp_ref[...]
    kbuf[NA_HALO:NA_HALO + TILE, :] = kc_ref[...]
    kbuf[NA_HALO + TILE:, :] = kn_ref[...]
    vbuf[0:NA_HALO, :] = vp_ref[...]
    vbuf[NA_HALO:NA_HALO + TILE, :] = vc_ref[...]
    vbuf[NA_HALO + TILE:, :] = vn_ref[...]
    qk_scale = HEAD_DIM ** -0.5 * LOG2E

    trips = TILE // (NA_Q * NA_UNROLL)
    ep_rows = ob_ref.shape[0] // trips

    def body(it, carry):
        _gla_epilogue_rows(pl.ds(pl.multiple_of(it * ep_rows, ep_rows), ep_rows),
                           of_ref, obw_ref, og_ref, gain_ref, ob_ref)
        qs, off, pid = [], [], []
        for u in range(NA_UNROLL):
            grp = it * NA_UNROLL + u
            r = r0 + NA_Q_ROWS * grp
            start = jnp.clip(r - C_WIN_R // 2, row_lo, row_hi - NA_KEY_ROWS)
            off.append(pl.multiple_of((start - r0 + NA_HALO // GRID_W) * GRID_W, NA_Q))
            pid.append(jnp.where(r == row_lo, 0, jnp.where(r == row_hi - NA_Q_ROWS, 2, 1)))
            qs.append(pl.multiple_of(grp * NA_Q, NA_Q))
        t = [_nt_dot(q_ref[pl.ds(qs[u], NA_Q), :], kbuf[pl.ds(off[u], NA_KEYS), :]) * qk_scale + tab_ref[0, pid[u]]
             for u in range(NA_UNROLL)]
        p, den = [], []
        for u in range(NA_UNROLL):
            e = jnp.exp2(t[u] - jnp.max(t[u], axis=-1, keepdims=True))
            den.append(jnp.sum(e, axis=-1, keepdims=True))
            p.append(e.astype(BF16))
        for u in range(NA_UNROLL):
            o = jnp.dot(p[u], vbuf[pl.ds(off[u], NA_KEYS), :], preferred_element_type=F32) / den[u]
            o_ref[pl.ds(qs[u], NA_Q), :] = o.astype(o_ref.dtype)
        return carry
    lax.fori_loop(0, trips, body, 0)


def _na_attn(proj, table, o_f, o_bw, gain, seqs):
    t = proj.shape[0]
    nt = t // TILE
    per = TILE // NA_HALO
    last = t // NA_HALO - 1
    qblk, kblk, vblk = P_C_Q // HEAD_DIM, P_C_K // HEAD_DIM, P_C_V // HEAD_DIM
    prev = lambda c0: (lambda h, g: (jnp.maximum(g * per - 1, 0), c0 + h))
    cur = lambda c0: (lambda h, g: (g, c0 + h))
    nxt = lambda c0: (lambda h, g: (jnp.minimum((g + 1) * per, last), c0 + h))
    ep = TILE // C_HEADS
    ep_map = lambda c: (lambda h, g: (g * C_HEADS + h, c))
    return pl.pallas_call(
        functools.partial(_na_kernel, seqs),
        grid=(C_HEADS, nt),
        in_specs=[
            pl.BlockSpec((TILE, HEAD_DIM), cur(qblk)),
            pl.BlockSpec((NA_HALO, HEAD_DIM), prev(kblk)),
            pl.BlockSpec((TILE, HEAD_DIM), cur(kblk)),
            pl.BlockSpec((NA_HALO, HEAD_DIM), nxt(kblk)),
            pl.BlockSpec((NA_HALO, HEAD_DIM), prev(vblk)),
            pl.BlockSpec((TILE, HEAD_DIM), cur(vblk)),
            pl.BlockSpec((NA_HALO, HEAD_DIM), nxt(vblk)),
            pl.BlockSpec((1, NA_PATTERNS, NA_Q, NA_KEYS), lambda h, g: (h, 0, 0, 0)),
            pl.BlockSpec((ep, B_V), ep_map(0)),
            pl.BlockSpec((ep, B_V), ep_map(0)),
            pl.BlockSpec((ep, B_V), ep_map(P_B_OG // B_V)),
            pl.BlockSpec((1, B_DV), lambda h, g: (0, 0)),
        ],
        out_specs=[pl.BlockSpec((TILE, HEAD_DIM), lambda h, g: (g, h)), pl.BlockSpec((ep, B_V), ep_map(0))],
        out_shape=[jax.ShapeDtypeStruct((t, C_W), BF16), jax.ShapeDtypeStruct((t, B_V), BF16)],
        scratch_shapes=[pltpu.VMEM((TILE + 2 * NA_HALO, HEAD_DIM), BF16),
                        pltpu.VMEM((TILE + 2 * NA_HALO, HEAD_DIM), BF16)],
        compiler_params=_cparams(("parallel", "parallel")),
        name="na_attn",
    )(proj, proj, proj, proj, proj, proj, proj, table, o_f, o_bw, proj, gain)


def _pack_w_in(w):
    sl = lambda off, n: w[:, off:off + n]
    cols = [sl(IN_A_Q, A_Q), sl(IN_B_V, B_V), sl(IN_B_OG, B_V), sl(IN_C_Q, C_W), sl(IN_C_K, C_W), sl(IN_C_V, C_W),
            sl(IN_B_Q, B_QK), sl(IN_B_K, B_QK), sl(IN_A_K, A_KV), sl(IN_A_V, A_KV)]
    wp = jnp.concatenate(cols, axis=1).astype(BF16)
    wgr = jnp.pad(sl(IN_B_GR, 2 * B_GATE_RANK), ((0, 0), (0, GR_PAD - 2 * B_GATE_RANK))).astype(BF16)
    wgate = sl(IN_GL, 3 * D_MODEL).astype(BF16)
    return wp, wgr, wgate


def _pack_gla_w2(w2):
    w2f = jnp.pad(w2[0], ((0, GR_PAD - B_GATE_RANK), (0, 0))).astype(BF16)
    w2b = jnp.pad(w2[1], ((B_GATE_RANK, GR_PAD - 2 * B_GATE_RANK), (0, 0))).astype(BF16)
    return w2f, w2b


def kernel(x_prompt, x_sample, norm1, w_in, sink_a, gla_w2, gla_b, gla_norm, rpb_c, w_br_a, w_br_b, w_br_c, w_out,
           norm2, w_ffn_in, w_ffn_out, norm_f):
    bp, sp, d = x_prompt.shape
    bs, ss, _ = x_sample.shape
    tp, ts = bp * sp, bs * ss
    assert sp % TILE == 0 and ss % TILE == 0 and d == D_MODEL
    seqs = (tp, sp, ss)
    t = tp + ts
    groups = ((x_prompt.reshape(tp, d), 0), (x_sample.reshape(ts, d), tp))
    x = None

    for l in range(DEPTH):
        wp, wgr, wgate = _pack_w_in(w_in[l])
        w2f, w2b = _pack_gla_w2(gla_w2[l])
        table = _na_bias_table(rpb_c[l])
        g1, g2 = norm1[l].reshape(1, d), norm2[l].reshape(1, d)
        if x is None:
            outs = None
            for xg, row0 in groups:
                outs = _inproj(xg, g1, wp, wgr, t, row0, outs)
            proj, gr, h = outs
        else:
            proj, gr, h = _inproj(x, g1, wp, wgr, t)
        o_a = _win_attn(proj, sink_a[l], seqs)
        o_f, o_bw = _gla(proj, gr, w2f, w2b, gla_b[l], seqs)
        o_c, o_b = _na_attn(proj, table, o_f, o_bw, gla_norm[l].reshape(1, B_DV), seqs)
        merged = _merge(o_a, o_b, o_c, h, wgate,
                        w_br_a[l].astype(BF16), w_br_b[l].astype(BF16), w_br_c[l].astype(BF16))
        if x is None:
            outs = None
            for xg, row0 in groups:
                outs = _out_proj(merged, w_out[l].astype(BF16), xg, g2, row0, outs)
            x, h2 = outs
        else:
            x, h2 = _out_proj(merged, w_out[l].astype(BF16), x, g2)
        act = _ffn_in(h2, w_ffn_in[l].astype(BF16))
        if l + 1 < DEPTH:
            x = _mm_res(act, w_ffn_out[l].astype(BF16), x, 1024, 512, "ffn_out")

    gf = norm_f.reshape(1, d)
    wfo = w_ffn_out[DEPTH - 1].astype(BF16)
    y_p = _ffn_out_norm(act, wfo, x, gf, 0, tp).reshape(bp, sp, d)
    y_s = _ffn_out_norm(act, wfo, x, gf, tp, ts).reshape(bs, ss, d)
    return (y_p, y_s)
```

```python
import functools

import numpy as np
import jax
import jax.numpy as jnp
from jax import lax
from jax.experimental import pallas as pl
from jax.experimental.pallas import tpu as pltpu

F32 = jnp.float32
BF16 = jnp.bfloat16

D_MODEL = 2048
DEPTH = 2
GRID_W = 64
HEAD_DIM = 128
A_HEADS = 8
A_KV_HEADS = 2
A_GROUP = A_HEADS // A_KV_HEADS
A_WINDOW = 128
A_BLOCK = 128
B_HEADS = 4
B_DK = 128
B_DV = 256
B_CHUNK = 64
B_GATE_RANK = 16
B_GATE_TAU = 16.0
C_HEADS = 8
C_WIN_R = 8
C_WIN_C = 16
D_FF = -(-8 * D_MODEL // (3 * 256)) * 256
RMS_EPS = 1e-6
NEG_INF = -1e30
LOG2E = 1.4426950408889634

A_Q = A_HEADS * HEAD_DIM
A_KV = A_KV_HEADS * HEAD_DIM
B_QK = B_HEADS * B_DK
B_V = B_HEADS * B_DV
C_W = C_HEADS * HEAD_DIM
SPLIT_SIZES = (A_Q, A_KV, A_KV, B_QK, B_QK, B_V, B_V, 2 * B_GATE_RANK, C_W, C_W, C_W, 3 * D_MODEL)
SPLIT_OFF = tuple(int(i) for i in np.cumsum((0,) + SPLIT_SIZES))
(IN_A_Q, IN_A_K, IN_A_V, IN_B_Q, IN_B_K, IN_B_V, IN_B_OG, IN_B_GR, IN_C_Q, IN_C_K, IN_C_V, IN_GL) = SPLIT_OFF[:-1]

P_A_Q = 0
P_B_V = 1024
P_B_OG = 2048
P_C_Q = 3072
P_C_K = 4096
P_C_V = 5120
P_B_Q = 6144
P_B_K = 6656
P_A_K = 7168
P_A_V = 7424
P_COLS = 7680
GR_PAD = 128

TILE = 2048
NA_Q_ROWS = 4
NA_KEY_ROWS = 12
NA_Q = NA_Q_ROWS * GRID_W
NA_KEYS = NA_KEY_ROWS * GRID_W
NA_HALO = 256
NA_UNROLL = 4

VMEM_LIMIT = 56 * 1024 * 1024


def _cparams(sem):
    return pltpu.CompilerParams(dimension_semantics=sem, vmem_limit_bytes=VMEM_LIMIT)


def _seq_bounds(pos, seqs):
    tp, sp, ss = seqs
    in_p = pos < tp
    lo_p = lax.div(pos, sp) * sp
    lo_s = tp + lax.div(jnp.maximum(pos - tp, 0), ss) * ss
    lo = jnp.where(in_p, lo_p, lo_s)
    hi = lo + jnp.where(in_p, sp, ss)
    return lo, hi


NORM_ROWS = 256


def _rmsnorm_rows(x_ref, g_ref, h_ref):
    def body(c, carry):
        r = pl.multiple_of(c * NORM_ROWS, NORM_ROWS)
        xs = x_ref[pl.ds(r, NORM_ROWS), :]
        ms = jnp.mean(xs * xs, axis=-1, keepdims=True)
        h_ref[pl.ds(r, NORM_ROWS), :] = (xs * lax.rsqrt(ms + RMS_EPS) * g_ref[...]).astype(h_ref.dtype)
        return carry
    lax.fori_loop(0, x_ref.shape[0] // NORM_ROWS, body, 0)


def _carry_rows(prev, n_in):
    if prev is None:
        return [], [], {}
    prev = list(prev)
    return prev, [pl.BlockSpec(memory_space=pl.ANY)] * len(prev), {n_in + k: k for k in range(len(prev))}


def _inproj_kernel(x_ref, g_ref, w_ref, wgr_ref, *rest):
    proj_ref, gr_ref, h_ref = rest[-3:]

    @pl.when(pl.program_id(1) == 0)
    def _():
        _rmsnorm_rows(x_ref, g_ref, h_ref)
        gr_ref[...] = jnp.dot(h_ref[...], wgr_ref[...], preferred_element_type=F32)
    proj_ref[...] = jnp.dot(h_ref[...], w_ref[...], preferred_element_type=F32).astype(proj_ref.dtype)


def _inproj(x, g, w, wgr, t_total, row0=0, prev=None, bm=1024, bn=1536):
    t, d = x.shape
    n = w.shape[1]
    blk0 = row0 // bm
    extra, extra_specs, aliases = _carry_rows(prev, 4)
    return pl.pallas_call(
        _inproj_kernel,
        grid=(t // bm, n // bn),
        in_specs=[
            pl.BlockSpec((bm, d), lambda i, j: (i, 0)),
            pl.BlockSpec((1, d), lambda i, j: (0, 0)),
            pl.BlockSpec((d, bn), lambda i, j: (0, j)),
            pl.BlockSpec((d, GR_PAD), lambda i, j: (0, 0)),
        ] + extra_specs,
        out_specs=[
            pl.BlockSpec((bm, bn), lambda i, j: (blk0 + i, j)),
            pl.BlockSpec((bm, GR_PAD), lambda i, j: (blk0 + i, 0)),
            pl.BlockSpec((bm, d), lambda i, j: (blk0 + i, 0)),
        ],
        out_shape=[jax.ShapeDtypeStruct((t_total, n), BF16), jax.ShapeDtypeStruct((t_total, GR_PAD), F32),
                   jax.ShapeDtypeStruct((t_total, d), BF16)],
        input_output_aliases=aliases,
        compiler_params=_cparams(("parallel", "arbitrary")),
        name="inproj",
    )(x, g, w, wgr, *extra)


def _ffn_in_kernel(h_ref, wg_ref, wu_ref, act_ref):
    h = h_ref[...]
    gate = jnp.dot(h, wg_ref[...], preferred_element_type=F32)
    up = jnp.dot(h, wu_ref[...], preferred_element_type=F32)
    act_ref[...] = (gate * jax.nn.sigmoid(gate) * up).astype(act_ref.dtype)


def _ffn_in(h, w, bm=1024, bn=512):
    t, d = h.shape
    ff = w.shape[1] // 2
    nj = ff // bn
    return pl.pallas_call(
        _ffn_in_kernel,
        grid=(t // bm, nj),
        in_specs=[
            pl.BlockSpec((bm, d), lambda i, j: (i, 0)),
            pl.BlockSpec((d, bn), lambda i, j: (0, j)),
            pl.BlockSpec((d, bn), lambda i, j: (0, nj + j)),
        ],
        out_specs=pl.BlockSpec((bm, bn), lambda i, j: (i, j)),
        out_shape=jax.ShapeDtypeStruct((t, ff), BF16),
        compiler_params=_cparams(("parallel", "arbitrary")),
        name="ffn_in",
    )(h, w, w)


def _out_proj_kernel(a_ref, w_ref, x_ref, g_ref, *rest):
    xo_ref, h_ref = rest[-2:]
    xo_ref[...] = x_ref[...] + jnp.dot(a_ref[...], w_ref[...], preferred_element_type=F32)
    _rmsnorm_rows(xo_ref, g_ref, h_ref)


def _out_proj(a, w, x, g, row0=0, prev=None, bm=512):
    t_total, k = a.shape
    t, d = x.shape
    blk0 = row0 // bm
    extra, extra_specs, aliases = _carry_rows(prev, 4)
    return pl.pallas_call(
        _out_proj_kernel,
        grid=(t // bm,),
        in_specs=[
            pl.BlockSpec((bm, k), lambda i: (blk0 + i, 0)),
            pl.BlockSpec((k, d), lambda i: (0, 0)),
            pl.BlockSpec((bm, d), lambda i: (i, 0)),
            pl.BlockSpec((1, d), lambda i: (0, 0)),
        ] + extra_specs,
        out_specs=[pl.BlockSpec((bm, d), lambda i: (blk0 + i, 0)), pl.BlockSpec((bm, d), lambda i: (blk0 + i, 0))],
        out_shape=[jax.ShapeDtypeStruct((t_total, d), F32), jax.ShapeDtypeStruct((t_total, d), BF16)],
        input_output_aliases=aliases,
        compiler_params=_cparams(("parallel",)),
        name="out_proj",
    )(a, w, x, g, *extra)


def _mm_res_kernel(a_ref, w_ref, x_ref, o_ref):
    o_ref[...] = x_ref[...] + jnp.dot(a_ref[...], w_ref[...], preferred_element_type=F32)


def _mm_res(a, w, x, bm, bn, name):
    t, k = a.shape
    n = w.shape[1]
    return pl.pallas_call(
        _mm_res_kernel,
        grid=(t // bm, n // bn),
        in_specs=[
            pl.BlockSpec((bm, k), lambda i, j: (i, 0)),
            pl.BlockSpec((k, bn), lambda i, j: (0, j)),
            pl.BlockSpec((bm, bn), lambda i, j: (i, j)),
        ],
        out_specs=pl.BlockSpec((bm, bn), lambda i, j: (i, j)),
        out_shape=jax.ShapeDtypeStruct((t, n), F32),
        compiler_params=_cparams(("parallel", "arbitrary")),
        name=name,
    )(a, w, x)


def _ffn_out_norm_kernel(a_ref, w_ref, x_ref, g_ref, y_ref):
    j = pl.program_id(1)
    bn = w_ref.shape[1]
    c0 = pl.multiple_of(j * bn, bn)
    y_ref[:, pl.ds(c0, bn)] = x_ref[...] + jnp.dot(a_ref[...], w_ref[...], preferred_element_type=F32)

    @pl.when(j == pl.num_programs(1) - 1)
    def _():
        _rmsnorm_rows(y_ref, g_ref, y_ref)


def _ffn_out_norm(a, w, x, g, row0, rows, bm=512, bn=512):
    k = a.shape[1]
    d = w.shape[1]
    blk0 = row0 // bm
    return pl.pallas_call(
        _ffn_out_norm_kernel,
        grid=(rows // bm, d // bn),
        in_specs=[
            pl.BlockSpec((bm, k), lambda i, j: (blk0 + i, 0)),
            pl.BlockSpec((k, bn), lambda i, j: (0, j)),
            pl.BlockSpec((bm, bn), lambda i, j: (blk0 + i, j)),
            pl.BlockSpec((1, d), lambda i, j: (0, 0)),
        ],
        out_specs=pl.BlockSpec((bm, d), lambda i, j: (i, 0)),
        out_shape=jax.ShapeDtypeStruct((rows, d), F32),
        compiler_params=_cparams(("parallel", "arbitrary")),
        name="ffn_out_norm",
    )(a, w, x, g)


def _gla_epilogue_rows(rows, of_ref, obw_ref, og_ref, gain_ref, ob_ref):
    o = of_ref[rows, :].astype(F32) + obw_ref[rows, :].astype(F32)
    og = og_ref[rows, :].astype(F32)
    for h in range(B_HEADS):
        sl = slice(h * B_DV, (h + 1) * B_DV)
        oh = o[:, sl]
        ms = jnp.mean(oh * oh, axis=-1, keepdims=True)
        ogh = og[:, sl]
        y = oh * lax.rsqrt(ms + RMS_EPS) * gain_ref[...] * (ogh * jax.nn.sigmoid(ogh))
        ob_ref[rows, sl] = y.astype(ob_ref.dtype)


def _merge_kernel(oa_ref, ob_ref, oc_ref, h_ref, wga_ref, wgb_ref, wgc_ref, wa_ref, wb_ref, wc_ref, out_ref):
    h = h_ref[...]
    ga = jax.nn.sigmoid(jnp.dot(h, wga_ref[...], preferred_element_type=F32))
    acc = ga * jnp.dot(oa_ref[...], wa_ref[...], preferred_element_type=F32)
    gb = jax.nn.sigmoid(jnp.dot(h, wgb_ref[...], preferred_element_type=F32))
    acc = acc + gb * jnp.dot(ob_ref[...], wb_ref[...], preferred_element_type=F32)
    gc = jax.nn.sigmoid(jnp.dot(h, wgc_ref[...], preferred_element_type=F32))
    acc = acc + gc * jnp.dot(oc_ref[...], wc_ref[...], preferred_element_type=F32)
    out_ref[...] = acc.astype(out_ref.dtype)


def _merge(o_a, o_b, o_c, h, wgate, wa, wb, wc, bm=1024, bn=512):
    t = o_a.shape[0]
    d = wa.shape[1]
    kb = o_a.shape[1]
    gstep = d // bn
    wide = lambda i, j: (i, 0)
    return pl.pallas_call(
        _merge_kernel,
        grid=(t // bm, d // bn),
        in_specs=[
            pl.BlockSpec((bm, kb), wide),
            pl.BlockSpec((bm, kb), wide),
            pl.BlockSpec((bm, kb), wide),
            pl.BlockSpec((bm, d), wide),
            pl.BlockSpec((d, bn), lambda i, j: (0, j)),
            pl.BlockSpec((d, bn), lambda i, j: (0, gstep + j)),
            pl.BlockSpec((d, bn), lambda i, j: (0, 2 * gstep + j)),
            pl.BlockSpec((kb, bn), lambda i, j: (0, j)),
            pl.BlockSpec((kb, bn), lambda i, j: (0, j)),
            pl.BlockSpec((kb, bn), lambda i, j: (0, j)),
        ],
        out_specs=pl.BlockSpec((bm, bn), lambda i, j: (i, j)),
        out_shape=jax.ShapeDtypeStruct((t, d), BF16),
        compiler_params=_cparams(("parallel", "arbitrary")),
        name="merge",
    )(o_a, o_b, o_c, h, wgate, wgate, wgate, wa, wb, wc)


A_KEYS = 3 * A_BLOCK
A_ROWS = A_GROUP * A_BLOCK


def _nt_dot(a, b):
    return lax.dot_general(a, b, (((1,), (1,)), ((), ())), preferred_element_type=F32)


def _win_kernel(seqs, sink_ref, q_ref, kp_ref, kc_ref, kn_ref, vp_ref, vc_ref, vn_ref, o_ref, kbuf, vbuf):
    tile_start = pl.program_id(0) * TILE
    seq_lo, seq_hi = _seq_bounds(tile_start, seqs)
    kbuf[0:A_BLOCK, :] = kp_ref[...]
    kbuf[A_BLOCK:A_BLOCK + TILE, :] = kc_ref[...]
    kbuf[A_BLOCK + TILE:, :] = kn_ref[...]
    vbuf[0:A_BLOCK, :] = vp_ref[...]
    vbuf[A_BLOCK:A_BLOCK + TILE, :] = vc_ref[...]
    vbuf[A_BLOCK + TILE:, :] = vn_ref[...]

    row = lax.broadcasted_iota(jnp.int32, (A_ROWS, A_KEYS), 0)
    col = lax.broadcasted_iota(jnp.int32, (A_ROWS, A_KEYS), 1)
    dist = jnp.abs((row % A_BLOCK) - (col - A_BLOCK))
    in_window = dist <= A_WINDOW
    distf = dist.astype(F32)
    grp = lax.broadcasted_iota(jnp.int32, (A_ROWS, 1), 0) // A_BLOCK
    kcol = lax.broadcasted_iota(jnp.int32, (1, A_KEYS), 1)
    qk_scale = HEAD_DIM ** -0.5 * LOG2E

    nbias, sink = [], []
    for kvh in range(A_KV_HEADS):
        slope = jnp.zeros((A_ROWS, 1), F32)
        snk = jnp.zeros((A_ROWS, 1), F32)
        for g in range(A_GROUP):
            h = kvh * A_GROUP + g
            slope = jnp.where(grp == g, 2.0 ** (-8.0 * (h + 1) / A_HEADS) * LOG2E, slope)
            snk = jnp.where(grp == g, sink_ref[h] * LOG2E, snk)
        nbias.append(jnp.where(in_window, -slope * distf, NEG_INF))
        sink.append(snk)

    def body(n, carry):
        r = pl.multiple_of(n * A_BLOCK, A_BLOCK)
        kabs = tile_start - A_BLOCK + r + kcol
        kbias = jnp.where((kabs >= seq_lo) & (kabs < seq_hi), 0.0, NEG_INF)
        t = []
        for kvh in range(A_KV_HEADS):
            q4 = jnp.concatenate(
                [q_ref[pl.ds(r, A_BLOCK), (kvh * A_GROUP + g) * HEAD_DIM:(kvh * A_GROUP + g + 1) * HEAD_DIM]
                 for g in range(A_GROUP)], axis=0)
            kw = kbuf[pl.ds(r, A_KEYS), kvh * HEAD_DIM:(kvh + 1) * HEAD_DIM]
            t.append(_nt_dot(q4, kw) * qk_scale + nbias[kvh] + kbias)
        p, den = [], []
        for kvh in range(A_KV_HEADS):
            m = jnp.maximum(jnp.max(t[kvh], axis=-1, keepdims=True), sink[kvh])
            e = jnp.exp2(t[kvh] - m)
            den.append(jnp.sum(e, axis=-1, keepdims=True) + jnp.exp2(sink[kvh] - m))
            p.append(e.astype(BF16))
        for kvh in range(A_KV_HEADS):
            vw = vbuf[pl.ds(r, A_KEYS), kvh * HEAD_DIM:(kvh + 1) * HEAD_DIM]
            o = jnp.dot(p[kvh], vw, preferred_element_type=F32) / den[kvh]
            for g in range(A_GROUP):
                h = kvh * A_GROUP + g
                o_ref[pl.ds(r, A_BLOCK), h * HEAD_DIM:(h + 1) * HEAD_DIM] = (
                    o[g * A_BLOCK:(g + 1) * A_BLOCK].astype(o_ref.dtype))
        return carry
    lax.fori_loop(0, TILE // A_BLOCK, body, 0)


def _win_attn(proj, sink, seqs):
    t = proj.shape[0]
    nt = t // TILE
    per = TILE // A_BLOCK
    last = t // A_BLOCK - 1
    kblk = P_A_K // A_KV
    vblk = P_A_V // A_KV
    prev = lambda col: (lambda g: (jnp.maximum(g * per - 1, 0), col))
    cur = lambda col: (lambda g: (g, col))
    nxt = lambda col: (lambda g: (jnp.minimum((g + 1) * per, last), col))
    return pl.pallas_call(
        functools.partial(_win_kernel, seqs),
        grid=(nt,),
        in_specs=[
            pl.BlockSpec(memory_space=pltpu.SMEM),
            pl.BlockSpec((TILE, A_Q), lambda g: (g, P_A_Q // A_Q)),
            pl.BlockSpec((A_BLOCK, A_KV), prev(kblk)),
            pl.BlockSpec((TILE, A_KV), cur(kblk)),
            pl.BlockSpec((A_BLOCK, A_KV), nxt(kblk)),
            pl.BlockSpec((A_BLOCK, A_KV), prev(vblk)),
            pl.BlockSpec((TILE, A_KV), cur(vblk)),
            pl.BlockSpec((A_BLOCK, A_KV), nxt(vblk)),
        ],
        out_specs=pl.BlockSpec((TILE, A_Q), lambda g: (g, 0)),
        out_shape=jax.ShapeDtypeStruct((t, A_Q), BF16),
        scratch_shapes=[pltpu.VMEM((TILE + 2 * A_BLOCK, A_KV), BF16),
                        pltpu.VMEM((TILE + 2 * A_BLOCK, A_KV), BF16)],
        compiler_params=_cparams(("parallel",)),
        name="win_attn",
    )(sink, proj, proj, proj, proj, proj, proj, proj)


GLA_TILE = 1024
GLA_BLK = 256


def _tn_dot(a, b):
    return lax.dot_general(a, b, (((0,), (0,)), ((), ())), preferred_element_type=F32)


def _gla_prepare(dirs, bias_ref):
    row = lax.broadcasted_iota(jnp.int32, (GLA_BLK, GLA_BLK), 0)
    col = lax.broadcasted_iota(jnp.int32, (GLA_BLK, GLA_BLK), 1)
    same = (row // B_CHUNK) == (col // B_CHUNK)
    tris = [same & ((row >= col) if d[-1] else (row <= col)) for d in dirs]
    tris_bf = [t.astype(BF16) for t in tris]
    biases = [bias_ref[n:n + 1, :] for n in range(len(dirs))]

    def body(i, carry):
        r = pl.multiple_of(i * GLA_BLK, GLA_BLK)
        rows = pl.ds(r, GLA_BLK)
        his, los = [], []
        for (q_ref, k_ref, v_ref, gr_ref, w2_ref, qt_ref, kd_ref, dec_ref, oi_ref, _), bias in zip(dirs, biases):
            y = (jnp.dot(gr_ref[rows, :].astype(BF16), w2_ref[...], preferred_element_type=F32) + bias) * LOG2E
            lg = (jnp.minimum(y, 0.0) - jnp.log2(1.0 + jnp.exp2(-jnp.abs(y)))) * (1.0 / B_GATE_TAU)
            hi = lg.astype(BF16)
            his.append(hi)
            los.append((lg - hi.astype(F32)).astype(BF16))
        bs, tots = [], []
        for n in range(len(dirs)):
            b = (jnp.dot(tris_bf[n], his[n], preferred_element_type=F32)
                 + jnp.dot(tris_bf[n], los[n], preferred_element_type=F32))
            bs.append(b)
            bc = b.reshape(GLA_BLK // B_CHUNK, B_CHUNK, B_QK)
            tots.append(jnp.broadcast_to(jnp.min(bc, axis=1, keepdims=True), bc.shape).reshape(GLA_BLK, B_QK))
        qts, kts = [], []
        for n, (q_ref, k_ref, v_ref, gr_ref, w2_ref, qt_ref, kd_ref, dec_ref, oi_ref, _) in enumerate(dirs):
            b, tot = bs[n], tots[n]
            q = q_ref[rows, :].astype(F32) * (B_DK ** -0.5)
            k = k_ref[rows, :].astype(F32)
            qt = (q * jnp.exp2(b)).astype(BF16)
            qts.append(qt)
            kts.append((k * jnp.exp2(-b)).astype(BF16))
            qt_ref[rows, :] = qt
            kd_ref[rows, :] = (k * jnp.exp2(tot - b)).astype(BF16)
            dec_ref[rows, :] = jnp.exp2(tot)
        for h in range(B_HEADS):
            ks = slice(h * B_DK, (h + 1) * B_DK)
            vs = slice(h * B_DV, (h + 1) * B_DV)
            for n, d in enumerate(dirs):
                v_ref, oi_ref = d[2], d[8]
                a = jnp.where(tris[n], _nt_dot(qts[n][:, ks], kts[n][:, ks]), 0.0).astype(BF16)
                oi_ref[rows, vs] = jnp.dot(a, v_ref[rows, vs], preferred_element_type=F32)
        return carry
    lax.fori_loop(0, GLA_TILE // GLA_BLK, body, 0)


def _gla_scan_chunks(chunks):
    work = [(c, h) for h in range(B_HEADS) for c in chunks]
    sts = [c[7][h] for c, h in work]
    for (c, h), st in zip(work, sts):
        r, v_ref, qt_ref, kd_ref, dec_ref, oi_ref, o_ref, st_ref = c
        rows = pl.ds(r, B_CHUNK)
        ks = slice(h * B_DK, (h + 1) * B_DK)
        vs = slice(h * B_DV, (h + 1) * B_DV)
        o = oi_ref[rows, vs] + _nt_dot(qt_ref[rows, ks], st.astype(BF16))
        o_ref[rows, vs] = o.astype(o_ref.dtype)
    for (c, h), st in zip(work, sts):
        r, v_ref, qt_ref, kd_ref, dec_ref, oi_ref, o_ref, st_ref = c
        rows = pl.ds(r, B_CHUNK)
        ks = slice(h * B_DK, (h + 1) * B_DK)
        vs = slice(h * B_DV, (h + 1) * B_DV)
        st_ref[h] = st * dec_ref[pl.ds(r, 1), ks] + _tn_dot(v_ref[rows, vs], kd_ref[rows, ks])


def _gla_kernel(seqs, qf_ref, kf_ref, vf_ref, grf_ref, qb_ref, kb_ref, vb_ref, grb_ref, w2f_ref, w2b_ref,
                bias_ref, of_ref, ob_ref, sf_ref, sb_ref, qtf, kdf, decf, oif, qtb, kdb, decb, oib):
    i = pl.program_id(0)
    nt = pl.num_programs(0)
    f_start = i * GLA_TILE
    b_start = (nt - 1 - i) * GLA_TILE
    f_lo, _ = _seq_bounds(f_start, seqs)
    _, b_hi = _seq_bounds(b_start, seqs)

    @pl.when(f_start == f_lo)
    def _():
        sf_ref[...] = jnp.zeros_like(sf_ref)

    @pl.when(b_start + GLA_TILE == b_hi)
    def _():
        sb_ref[...] = jnp.zeros_like(sb_ref)

    _gla_prepare([(qf_ref, kf_ref, vf_ref, grf_ref, w2f_ref, qtf, kdf, decf, oif, True),
                  (qb_ref, kb_ref, vb_ref, grb_ref, w2b_ref, qtb, kdb, decb, oib, False)], bias_ref)

    nchunk = GLA_TILE // B_CHUNK

    def body(c, carry):
        rf = pl.multiple_of(c * B_CHUNK, B_CHUNK)
        rb = pl.multiple_of((nchunk - 1 - c) * B_CHUNK, B_CHUNK)
        _gla_scan_chunks([(rf, vf_ref, qtf, kdf, decf, oif, of_ref, sf_ref),
                          (rb, vb_ref, qtb, kdb, decb, oib, ob_ref, sb_ref)])
        return carry
    lax.fori_loop(0, nchunk, body, 0, unroll=4)


def _gla(proj, gr, w2f, w2b, bias, seqs):
    t = proj.shape[0]
    nt = t // GLA_TILE
    fwd = lambda col: (lambda i: (i, col))
    bwd = lambda col: (lambda i: (nt - 1 - i, col))
    qblk, kblk, vblk = P_B_Q // B_QK, P_B_K // B_QK, P_B_V // B_V

    def specs(mk):
        return [pl.BlockSpec((GLA_TILE, B_QK), mk(qblk)), pl.BlockSpec((GLA_TILE, B_QK), mk(kblk)),
                pl.BlockSpec((GLA_TILE, B_V), mk(vblk)), pl.BlockSpec((GLA_TILE, GR_PAD), mk(0))]

    const = lambda i: (0, 0)
    state = pltpu.VMEM((B_HEADS, B_DV, B_DK), F32)
    per_dir = [pltpu.VMEM((GLA_TILE, B_QK), BF16), pltpu.VMEM((GLA_TILE, B_QK), BF16),
               pltpu.VMEM((GLA_TILE, B_QK), F32), pltpu.VMEM((GLA_TILE, B_V), F32)]
    return pl.pallas_call(
        functools.partial(_gla_kernel, seqs),
        grid=(nt,),
        in_specs=specs(fwd) + specs(bwd) + [
            pl.BlockSpec((GR_PAD, B_QK), const), pl.BlockSpec((GR_PAD, B_QK), const),
            pl.BlockSpec((2, B_QK), const)],
        out_specs=[pl.BlockSpec((GLA_TILE, B_V), fwd(0)), pl.BlockSpec((GLA_TILE, B_V), bwd(0))],
        out_shape=[jax.ShapeDtypeStruct((t, B_V), BF16), jax.ShapeDtypeStruct((t, B_V), BF16)],
        scratch_shapes=[state, state] + per_dir + per_dir,
        compiler_params=_cparams(("arbitrary",)),
        name="gla",
    )(proj, proj, proj, gr, proj, proj, proj, gr, w2f, w2b, bias)


NA_PATTERNS = 3
_NA_ROWS_FOR_TABLE = 32
_NA_PATTERN_ROWS = (0, 2 * NA_Q_ROWS, _NA_ROWS_FOR_TABLE - NA_Q_ROWS)


def _na_bias_table(rpb):
    R = _NA_ROWS_FOR_TABLE
    kr = C_WIN_R
    key_rows = NA_KEY_ROWS
    rpb = rpb.astype(F32) * LOG2E
    n_dr = rpb.shape[1]
    edge = GRID_W - C_WIN_C
    ext = jnp.concatenate([jnp.repeat(rpb[:, :, :1], edge, axis=2), rpb, jnp.repeat(rpb[:, :, -1:], edge + 1, axis=2)], axis=2)
    skew = jnp.tile(ext, (1, 1, GRID_W))[:, :, :GRID_W * (2 * GRID_W - 1)].reshape(C_HEADS, n_dr, GRID_W, 2 * GRID_W - 1)
    col = skew[:, :, :, GRID_W - 1:]
    cq = np.arange(GRID_W)[:, None]
    ck = np.arange(GRID_W)[None, :]
    col_start = np.clip(cq - C_WIN_C // 2, 0, GRID_W - C_WIN_C)
    col_ok = (ck >= col_start) & (ck < col_start + C_WIN_C)
    col = jnp.where(col_ok[None, None], col, NEG_INF)
    colp = jnp.pad(col.transpose(0, 2, 1, 3), ((0, 0), (0, 0), (key_rows - kr, key_rows - kr), (0, 0)))
    colp = colp.reshape(C_HEADS, GRID_W, (n_dr + 2 * (key_rows - kr)) * GRID_W)
    pieces, row_ok = [], np.zeros((NA_PATTERNS * NA_Q_ROWS, 1, key_rows, GRID_W), bool)
    for r in _NA_PATTERN_ROWS:
        start = int(np.clip(r - kr // 2, 0, R - key_rows))
        for jj in range(NA_Q_ROWS):
            qrow = r + jj
            ws = int(np.clip(qrow - kr // 2, 0, R - kr))
            lead, a0 = ws - start, ws - qrow + (C_WIN_R - 1)
            assert 0 <= lead <= key_rows - kr and 0 <= a0 <= n_dr - kr
            row_ok[len(pieces), 0, lead:lead + kr] = True
            first = a0 - lead + key_rows - kr
            pieces.append(colp[:, :, first * GRID_W:(first + key_rows) * GRID_W])
    tab = jnp.stack(pieces, axis=1)
    tab = jnp.where(row_ok.reshape(1, NA_PATTERNS * NA_Q_ROWS, 1, NA_KEYS), tab, NEG_INF)
    return tab.reshape(C_HEADS, NA_PATTERNS, NA_Q, NA_KEYS)


def _na_kernel(seqs, q_ref, kp_ref, kc_ref, kn_ref, vp_ref, vc_ref, vn_ref, tab_ref, of_ref, obw_ref, og_ref,
               gain_ref, o_ref, ob_ref, kbuf, vbuf):
    tile_start = pl.program_id(1) * TILE
    seq_lo, seq_hi = _seq_bounds(tile_start, seqs)
    r0 = lax.div(tile_start, GRID_W)
    row_lo = lax.div(seq_lo, GRID_W)
    row_hi = lax.div(seq_hi, GRID_W)
    kbuf[0:NA_HALO, :] = kp_ref[...]
    kbuf[NA_HALO:NA_HALO + TILE, :] = kc_ref[...]
    kbuf[NA_HALO + TILE:, :] = kn_ref[...]
    vbuf[0:NA_HALO, :] = vp_ref[...]
    vbuf[NA_HALO:NA_HALO + TILE, :] = vc_ref[...]
    vbuf[NA_HALO + TILE:, :] = vn_ref[...]
    qk_scale = HEAD_DIM ** -0.5 * LOG2E

    trips = TILE // (NA_Q * NA_UNROLL)
    ep_rows = ob_ref.shape[0] // trips

    def body(it, carry):
        _gla_epilogue_rows(pl.ds(pl.multiple_of(it * ep_rows, ep_rows), ep_rows),
                           of_ref, obw_ref, og_ref, gain_ref, ob_ref)
        qs, off, pid = [], [], []
        for u in range(NA_UNROLL):
            grp = it * NA_UNROLL + u
            r = r0 + NA_Q_ROWS * grp
            start = jnp.clip(r - C_WIN_R // 2, row_lo, row_hi - NA_KEY_ROWS)
            off.append(pl.multiple_of((start - r0 + NA_HALO // GRID_W) * GRID_W, NA_Q))
            pid.append(jnp.where(r == row_lo, 0, jnp.where(r == row_hi - NA_Q_ROWS, 2, 1)))
            qs.append(pl.multiple_of(grp * NA_Q, NA_Q))
        t = [_nt_dot(q_ref[pl.ds(qs[u], NA_Q), :], kbuf[pl.ds(off[u], NA_KEYS), :]) * qk_scale + tab_ref[0, pid[u]]
             for u in range(NA_UNROLL)]
        p, den = [], []
        for u in range(NA_UNROLL):
            e = jnp.exp2(t[u] - jnp.max(t[u], axis=-1, keepdims=True))
            den.append(jnp.sum(e, axis=-1, keepdims=True))
            p.append(e.astype(BF16))
        for u in range(NA_UNROLL):
            o = jnp.dot(p[u], vbuf[pl.ds(off[u], NA_KEYS), :], preferred_element_type=F32) / den[u]
            o_ref[pl.ds(qs[u], NA_Q), :] = o.astype(o_ref.dtype)
        return carry
    lax.fori_loop(0, trips, body, 0)


def _na_attn(proj, table, o_f, o_bw, gain, seqs):
    t = proj.shape[0]
    nt = t // TILE
    per = TILE // NA_HALO
    last = t // NA_HALO - 1
    qblk, kblk, vblk = P_C_Q // HEAD_DIM, P_C_K // HEAD_DIM, P_C_V // HEAD_DIM
    prev = lambda c0: (lambda h, g: (jnp.maximum(g * per - 1, 0), c0 + h))
    cur = lambda c0: (lambda h, g: (g, c0 + h))
    nxt = lambda c0: (lambda h, g: (jnp.minimum((g + 1) * per, last), c0 + h))
    ep = TILE // C_HEADS
    ep_map = lambda c: (lambda h, g: (g * C_HEADS + h, c))
    return pl.pallas_call(
        functools.partial(_na_kernel, seqs),
        grid=(C_HEADS, nt),
        in_specs=[
            pl.BlockSpec((TILE, HEAD_DIM), cur(qblk)),
            pl.BlockSpec((NA_HALO, HEAD_DIM), prev(kblk)),
            pl.BlockSpec((TILE, HEAD_DIM), cur(kblk)),
            pl.BlockSpec((NA_HALO, HEAD_DIM), nxt(kblk)),
            pl.BlockSpec((NA_HALO, HEAD_DIM), prev(vblk)),
            pl.BlockSpec((TILE, HEAD_DIM), cur(vblk)),
            pl.BlockSpec((NA_HALO, HEAD_DIM), nxt(vblk)),
            pl.BlockSpec((1, NA_PATTERNS, NA_Q, NA_KEYS), lambda h, g: (h, 0, 0, 0)),
            pl.BlockSpec((ep, B_V), ep_map(0)),
            pl.BlockSpec((ep, B_V), ep_map(0)),
            pl.BlockSpec((ep, B_V), ep_map(P_B_OG // B_V)),
            pl.BlockSpec((1, B_DV), lambda h, g: (0, 0)),
        ],
        out_specs=[pl.BlockSpec((TILE, HEAD_DIM), lambda h, g: (g, h)), pl.BlockSpec((ep, B_V), ep_map(0))],
        out_shape=[jax.ShapeDtypeStruct((t, C_W), BF16), jax.ShapeDtypeStruct((t, B_V), BF16)],
        scratch_shapes=[pltpu.VMEM((TILE + 2 * NA_HALO, HEAD_DIM), BF16),
                        pltpu.VMEM((TILE + 2 * NA_HALO, HEAD_DIM), BF16)],
        compiler_params=_cparams(("parallel", "parallel")),
        name="na_attn",
    )(proj, proj, proj, proj, proj, proj, proj, table, o_f, o_bw, proj, gain)


def _pack_w_in(w):
    sl = lambda off, n: w[:, off:off + n]
    cols = [sl(IN_A_Q, A_Q), sl(IN_B_V, B_V), sl(IN_B_OG, B_V), sl(IN_C_Q, C_W), sl(IN_C_K, C_W), sl(IN_C_V, C_W),
            sl(IN_B_Q, B_QK), sl(IN_B_K, B_QK), sl(IN_A_K, A_KV), sl(IN_A_V, A_KV)]
    wp = jnp.concatenate(cols, axis=1).astype(BF16)
    wgr = jnp.pad(sl(IN_B_GR, 2 * B_GATE_RANK), ((0, 0), (0, GR_PAD - 2 * B_GATE_RANK))).astype(BF16)
    wgate = sl(IN_GL, 3 * D_MODEL).astype(BF16)
    return wp, wgr, wgate


def _pack_gla_w2(w2):
    w2f = jnp.pad(w2[0], ((0, GR_PAD - B_GATE_RANK), (0, 0))).astype(BF16)
    w2b = jnp.pad(w2[1], ((B_GATE_RANK, GR_PAD - 2 * B_GATE_RANK), (0, 0))).astype(BF16)
    return w2f, w2b


def kernel(x_prompt, x_sample, norm1, w_in, sink_a, gla_w2, gla_b, gla_norm, rpb_c, w_br_a, w_br_b, w_br_c, w_out,
           norm2, w_ffn_in, w_ffn_out, norm_f):
    bp, sp, d = x_prompt.shape
    bs, ss, _ = x_sample.shape
    tp, ts = bp * sp, bs * ss
    assert sp % TILE == 0 and ss % TILE == 0 and d == D_MODEL
    seqs = (tp, sp, ss)
    t = tp + ts
    groups = ((x_prompt.reshape(tp, d), 0), (x_sample.reshape(ts, d), tp))
    x = None

    for l in range(DEPTH):
        wp, wgr, wgate = _pack_w_in(w_in[l])
        w2f, w2b = _pack_gla_w2(gla_w2[l])
        table = _na_bias_table(rpb_c[l])
        g1, g2 = norm1[l].reshape(1, d), norm2[l].reshape(1, d)
        if x is None:
            outs = None
            for xg, row0 in groups:
                outs = _inproj(xg, g1, wp, wgr, t, row0, outs)
            proj, gr, h = outs
        else:
            proj, gr, h = _inproj(x, g1, wp, wgr, t)
        o_a = _win_attn(proj, sink_a[l], seqs)
        o_f, o_bw = _gla(proj, gr, w2f, w2b, gla_b[l], seqs)
        o_c, o_b = _na_attn(proj, table, o_f, o_bw, gla_norm[l].reshape(1, B_DV), seqs)
        merged = _merge(o_a, o_b, o_c, h, wgate,
                        w_br_a[l].astype(BF16), w_br_b[l].astype(BF16), w_br_c[l].astype(BF16))
        if x is None:
            outs = None
            for xg, row0 in groups:
                outs = _out_proj(merged, w_out[l].astype(BF16), xg, g2, row0, outs)
            x, h2 = outs
        else:
            x, h2 = _out_proj(merged, w_out[l].astype(BF16), x, g2)
        act = _ffn_in(h2, w_ffn_in[l].astype(BF16))
        if l + 1 < DEPTH:
            x = _mm_res(act, w_ffn_out[l].astype(BF16), x, 1024, 512, "ffn_out")

    gf = norm_f.reshape(1, d)
    wfo = w_ffn_out[DEPTH - 1].astype(BF16)
    y_p = _ffn_out_norm(act, wfo, x, gf, 0, tp).reshape(bp, sp, d)
    y_s = _ffn_out_norm(act, wfo, x, gf, tp, ts).reshape(bs, ss, d)
    return (y_p, y_s)
```

```python
import functools

import numpy as np
import jax
import jax.numpy as jnp
from jax import lax
from jax.experimental import pallas as pl
from jax.experimental.pallas import tpu as pltpu

F32 = jnp.float32
BF16 = jnp.bfloat16

D_MODEL = 2048
DEPTH = 2
GRID_W = 64
HEAD_DIM = 128
A_HEADS = 8
A_KV_HEADS = 2
A_GROUP = A_HEADS // A_KV_HEADS
A_WINDOW = 128
A_BLOCK = 128
B_HEADS = 4
B_DK = 128
B_DV = 256
B_CHUNK = 64
B_GATE_RANK = 16
B_GATE_TAU = 16.0
C_HEADS = 8
C_WIN_R = 8
C_WIN_C = 16
D_FF = -(-8 * D_MODEL // (3 * 256)) * 256
RMS_EPS = 1e-6
NEG_INF = -1e30
LOG2E = 1.4426950408889634

A_Q = A_HEADS * HEAD_DIM
A_KV = A_KV_HEADS * HEAD_DIM
B_QK = B_HEADS * B_DK
B_V = B_HEADS * B_DV
C_W = C_HEADS * HEAD_DIM
SPLIT_SIZES = (A_Q, A_KV, A_KV, B_QK, B_QK, B_V, B_V, 2 * B_GATE_RANK, C_W, C_W, C_W, 3 * D_MODEL)
SPLIT_OFF = tuple(int(i) for i in np.cumsum((0,) + SPLIT_SIZES))
(IN_A_Q, IN_A_K, IN_A_V, IN_B_Q, IN_B_K, IN_B_V, IN_B_OG, IN_B_GR, IN_C_Q, IN_C_K, IN_C_V, IN_GL) = SPLIT_OFF[:-1]

P_A_Q = 0
P_B_V = 1024
P_B_OG = 2048
P_C_Q = 3072
P_C_K = 4096
P_C_V = 5120
P_B_Q = 6144
P_B_K = 6656
P_A_K = 7168
P_A_V = 7424
P_COLS = 7680
GR_PAD = 128

TILE = 2048
NA_Q_ROWS = 4
NA_KEY_ROWS = 12
NA_Q = NA_Q_ROWS * GRID_W
NA_KEYS = NA_KEY_ROWS * GRID_W
NA_HALO = 256
NA_UNROLL = 4

VMEM_LIMIT = 56 * 1024 * 1024


def _cparams(sem):
    return pltpu.CompilerParams(dimension_semantics=sem, vmem_limit_bytes=VMEM_LIMIT)


def _seq_bounds(pos, seqs):
    tp, sp, ss = seqs
    in_p = pos < tp
    lo_p = lax.div(pos, sp) * sp
    lo_s = tp + lax.div(jnp.maximum(pos - tp, 0), ss) * ss
    lo = jnp.where(in_p, lo_p, lo_s)
    hi = lo + jnp.where(in_p, sp, ss)
    return lo, hi


NORM_ROWS = 256


def _rmsnorm_rows(x_ref, g_ref, h_ref):
    def body(c, carry):
        r = pl.multiple_of(c * NORM_ROWS, NORM_ROWS)
        xs = x_ref[pl.ds(r, NORM_ROWS), :]
        ms = jnp.mean(xs * xs, axis=-1, keepdims=True)
        h_ref[pl.ds(r, NORM_ROWS), :] = (xs * lax.rsqrt(ms + RMS_EPS) * g_ref[...]).astype(h_ref.dtype)
        return carry
    lax.fori_loop(0, x_ref.shape[0] // NORM_ROWS, body, 0)


def _row_tile_specs(xs, bm, d):
    if len(xs) == 1:
        return [pl.BlockSpec((bm, d), lambda i, *_: (i, 0))], None
    na = xs[0].shape[0] // bm
    return [pl.BlockSpec((bm, d), lambda i, *_: (jnp.minimum(i, na - 1), 0)),
            pl.BlockSpec((bm, d), lambda i, *_: (jnp.maximum(i - na, 0), 0))], na


def _inproj_kernel(na, *refs):
    nx = 1 if na is None else 2
    x_refs = refs[:nx]
    g_ref, w_ref, wgr_ref, proj_ref, gr_ref, h_ref = refs[nx:]

    @pl.when(pl.program_id(1) == 0)
    def _():
        if na is None:
            _rmsnorm_rows(x_refs[0], g_ref, h_ref)
        else:
            @pl.when(pl.program_id(0) < na)
            def _():
                _rmsnorm_rows(x_refs[0], g_ref, h_ref)

            @pl.when(pl.program_id(0) >= na)
            def _():
                _rmsnorm_rows(x_refs[1], g_ref, h_ref)
        gr_ref[...] = jnp.dot(h_ref[...], wgr_ref[...], preferred_element_type=F32)
    proj_ref[...] = jnp.dot(h_ref[...], w_ref[...], preferred_element_type=F32).astype(proj_ref.dtype)


def _inproj(xs, g, w, wgr, bm, bn=1536):
    t = sum(x.shape[0] for x in xs)
    d = xs[0].shape[1]
    n = w.shape[1]
    x_specs, na = _row_tile_specs(xs, bm, d)
    return pl.pallas_call(
        functools.partial(_inproj_kernel, na),
        grid=(t // bm, n // bn),
        in_specs=x_specs + [
            pl.BlockSpec((1, d), lambda i, j: (0, 0)),
            pl.BlockSpec((d, bn), lambda i, j: (0, j)),
            pl.BlockSpec((d, GR_PAD), lambda i, j: (0, 0)),
        ],
        out_specs=[
            pl.BlockSpec((bm, bn), lambda i, j: (i, j)),
            pl.BlockSpec((bm, GR_PAD), lambda i, j: (i, 0)),
            pl.BlockSpec((bm, d), lambda i, j: (i, 0)),
        ],
        out_shape=[jax.ShapeDtypeStruct((t, n), BF16), jax.ShapeDtypeStruct((t, GR_PAD), F32),
                   jax.ShapeDtypeStruct((t, d), BF16)],
        compiler_params=_cparams(("parallel", "arbitrary")),
        name="inproj",
    )(*xs, g, w, wgr)


def _ffn_in_kernel(h_ref, wg_ref, wu_ref, act_ref):
    h = h_ref[...]
    gate = jnp.dot(h, wg_ref[...], preferred_element_type=F32)
    up = jnp.dot(h, wu_ref[...], preferred_element_type=F32)
    act_ref[...] = (gate * jax.nn.sigmoid(gate) * up).astype(act_ref.dtype)


def _ffn_in(h, w, bm=1024, bn=512):
    t, d = h.shape
    ff = w.shape[1] // 2
    nj = ff // bn
    return pl.pallas_call(
        _ffn_in_kernel,
        grid=(t // bm, nj),
        in_specs=[
            pl.BlockSpec((bm, d), lambda i, j: (i, 0)),
            pl.BlockSpec((d, bn), lambda i, j: (0, j)),
            pl.BlockSpec((d, bn), lambda i, j: (0, nj + j)),
        ],
        out_specs=pl.BlockSpec((bm, bn), lambda i, j: (i, j)),
        out_shape=jax.ShapeDtypeStruct((t, ff), BF16),
        compiler_params=_cparams(("parallel", "arbitrary")),
        name="ffn_in",
    )(h, w, w)


def _out_proj_kernel(na, a_ref, w_ref, *refs):
    nx = 1 if na is None else 2
    x_refs = refs[:nx]
    g_ref, xo_ref, h_ref = refs[nx:]
    y = jnp.dot(a_ref[...], w_ref[...], preferred_element_type=F32)
    if na is None:
        xo_ref[...] = x_refs[0][...] + y
    else:
        @pl.when(pl.program_id(0) < na)
        def _():
            xo_ref[...] = x_refs[0][...] + y

        @pl.when(pl.program_id(0) >= na)
        def _():
            xo_ref[...] = x_refs[1][...] + y
    _rmsnorm_rows(xo_ref, g_ref, h_ref)


def _out_proj(a, w, xs, g, bm=512):
    t, k = a.shape
    d = w.shape[1]
    x_specs, na = _row_tile_specs(xs, bm, d)
    return pl.pallas_call(
        functools.partial(_out_proj_kernel, na),
        grid=(t // bm,),
        in_specs=[
            pl.BlockSpec((bm, k), lambda i: (i, 0)),
            pl.BlockSpec((k, d), lambda i: (0, 0)),
        ] + x_specs + [pl.BlockSpec((1, d), lambda i: (0, 0))],
        out_specs=[pl.BlockSpec((bm, d), lambda i: (i, 0)), pl.BlockSpec((bm, d), lambda i: (i, 0))],
        out_shape=[jax.ShapeDtypeStruct((t, d), F32), jax.ShapeDtypeStruct((t, d), BF16)],
        compiler_params=_cparams(("parallel",)),
        name="out_proj",
    )(a, w, *xs, g)


def _mm_res_kernel(a_ref, w_ref, x_ref, o_ref):
    o_ref[...] = x_ref[...] + jnp.dot(a_ref[...], w_ref[...], preferred_element_type=F32)


def _mm_res(a, w, x, bm, bn, name):
    t, k = a.shape
    n = w.shape[1]
    return pl.pallas_call(
        _mm_res_kernel,
        grid=(t // bm, n // bn),
        in_specs=[
            pl.BlockSpec((bm, k), lambda i, j: (i, 0)),
            pl.BlockSpec((k, bn), lambda i, j: (0, j)),
            pl.BlockSpec((bm, bn), lambda i, j: (i, j)),
        ],
        out_specs=pl.BlockSpec((bm, bn), lambda i, j: (i, j)),
        out_shape=jax.ShapeDtypeStruct((t, n), F32),
        compiler_params=_cparams(("parallel", "arbitrary")),
        name=name,
    )(a, w, x)


def _ffn_out_norm_kernel(a_ref, w_ref, x_ref, g_ref, y_ref):
    j = pl.program_id(1)
    bn = w_ref.shape[1]
    c0 = pl.multiple_of(j * bn, bn)
    y_ref[:, pl.ds(c0, bn)] = x_ref[...] + jnp.dot(a_ref[...], w_ref[...], preferred_element_type=F32)

    @pl.when(j == pl.num_programs(1) - 1)
    def _():
        _rmsnorm_rows(y_ref, g_ref, y_ref)


def _ffn_out_norm(a, w, x, g, row0, rows, bm=512, bn=512):
    k = a.shape[1]
    d = w.shape[1]
    blk0 = row0 // bm
    return pl.pallas_call(
        _ffn_out_norm_kernel,
        grid=(rows // bm, d // bn),
        in_specs=[
            pl.BlockSpec((bm, k), lambda i, j: (blk0 + i, 0)),
            pl.BlockSpec((k, bn), lambda i, j: (0, j)),
            pl.BlockSpec((bm, bn), lambda i, j: (blk0 + i, j)),
            pl.BlockSpec((1, d), lambda i, j: (0, 0)),
        ],
        out_specs=pl.BlockSpec((bm, d), lambda i, j: (i, 0)),
        out_shape=jax.ShapeDtypeStruct((rows, d), F32),
        compiler_params=_cparams(("parallel", "arbitrary")),
        name="ffn_out_norm",
    )(a, w, x, g)


def _gla_epilogue_rows(rows, of_ref, obw_ref, og_ref, gain_ref, ob_ref):
    o = of_ref[rows, :].astype(F32) + obw_ref[rows, :].astype(F32)
    og = og_ref[rows, :].astype(F32)
    for h in range(B_HEADS):
        sl = slice(h * B_DV, (h + 1) * B_DV)
        oh = o[:, sl]
        ms = jnp.mean(oh * oh, axis=-1, keepdims=True)
        ogh = og[:, sl]
        y = oh * lax.rsqrt(ms + RMS_EPS) * gain_ref[...] * (ogh * jax.nn.sigmoid(ogh))
        ob_ref[rows, sl] = y.astype(ob_ref.dtype)


def _merge_kernel(oa_ref, ob_ref, oc_ref, h_ref, wga_ref, wgb_ref, wgc_ref, wa_ref, wb_ref, wc_ref, out_ref):
    h = h_ref[...]
    ga = jax.nn.sigmoid(jnp.dot(h, wga_ref[...], preferred_element_type=F32))
    acc = ga * jnp.dot(oa_ref[...], wa_ref[...], preferred_element_type=F32)
    gb = jax.nn.sigmoid(jnp.dot(h, wgb_ref[...], preferred_element_type=F32))
    acc = acc + gb * jnp.dot(ob_ref[...], wb_ref[...], preferred_element_type=F32)
    gc = jax.nn.sigmoid(jnp.dot(h, wgc_ref[...], preferred_element_type=F32))
    acc = acc + gc * jnp.dot(oc_ref[...], wc_ref[...], preferred_element_type=F32)
    out_ref[...] = acc.astype(out_ref.dtype)


def _merge(o_a, o_b, o_c, h, wgate, wa, wb, wc, bm=1024, bn=512):
    t = o_a.shape[0]
    d = wa.shape[1]
    kb = o_a.shape[1]
    gstep = d // bn
    wide = lambda i, j: (i, 0)
    return pl.pallas_call(
        _merge_kernel,
        grid=(t // bm, d // bn),
        in_specs=[
            pl.BlockSpec((bm, kb), wide),
            pl.BlockSpec((bm, kb), wide),
            pl.BlockSpec((bm, kb), wide),
            pl.BlockSpec((bm, d), wide),
            pl.BlockSpec((d, bn), lambda i, j: (0, j)),
            pl.BlockSpec((d, bn), lambda i, j: (0, gstep + j)),
            pl.BlockSpec((d, bn), lambda i, j: (0, 2 * gstep + j)),
            pl.BlockSpec((kb, bn), lambda i, j: (0, j)),
            pl.BlockSpec((kb, bn), lambda i, j: (0, j)),
            pl.BlockSpec((kb, bn), lambda i, j: (0, j)),
        ],
        out_specs=pl.BlockSpec((bm, bn), lambda i, j: (i, j)),
        out_shape=jax.ShapeDtypeStruct((t, d), BF16),
        compiler_params=_cparams(("parallel", "arbitrary")),
        name="merge",
    )(o_a, o_b, o_c, h, wgate, wgate, wgate, wa, wb, wc)


A_KEYS = 3 * A_BLOCK
A_ROWS = A_GROUP * A_BLOCK


def _nt_dot(a, b):
    return lax.dot_general(a, b, (((1,), (1,)), ((), ())), preferred_element_type=F32)


def _win_kernel(seqs, sink_ref, q_ref, kp_ref, kc_ref, kn_ref, vp_ref, vc_ref, vn_ref, o_ref, kbuf, vbuf):
    tile_start = pl.program_id(0) * TILE
    seq_lo, seq_hi = _seq_bounds(tile_start, seqs)
    kbuf[0:A_BLOCK, :] = kp_ref[...]
    kbuf[A_BLOCK:A_BLOCK + TILE, :] = kc_ref[...]
    kbuf[A_BLOCK + TILE:, :] = kn_ref[...]
    vbuf[0:A_BLOCK, :] = vp_ref[...]
    vbuf[A_BLOCK:A_BLOCK + TILE, :] = vc_ref[...]
    vbuf[A_BLOCK + TILE:, :] = vn_ref[...]

    row = lax.broadcasted_iota(jnp.int32, (A_ROWS, A_KEYS), 0)
    col = lax.broadcasted_iota(jnp.int32, (A_ROWS, A_KEYS), 1)
    dist = jnp.abs((row % A_BLOCK) - (col - A_BLOCK))
    in_window = dist <= A_WINDOW
    distf = dist.astype(F32)
    grp = lax.broadcasted_iota(jnp.int32, (A_ROWS, 1), 0) // A_BLOCK
    kcol = lax.broadcasted_iota(jnp.int32, (1, A_KEYS), 1)
    qk_scale = HEAD_DIM ** -0.5 * LOG2E

    nbias, sink = [], []
    for kvh in range(A_KV_HEADS):
        slope = jnp.zeros((A_ROWS, 1), F32)
        snk = jnp.zeros((A_ROWS, 1), F32)
        for g in range(A_GROUP):
            h = kvh * A_GROUP + g
            slope = jnp.where(grp == g, 2.0 ** (-8.0 * (h + 1) / A_HEADS) * LOG2E, slope)
            snk = jnp.where(grp == g, sink_ref[h] * LOG2E, snk)
        nbias.append(jnp.where(in_window, -slope * distf, NEG_INF))
        sink.append(snk)

    def body(n, carry):
        r = pl.multiple_of(n * A_BLOCK, A_BLOCK)
        kabs = tile_start - A_BLOCK + r + kcol
        kbias = jnp.where((kabs >= seq_lo) & (kabs < seq_hi), 0.0, NEG_INF)
        t = []
        for kvh in range(A_KV_HEADS):
            q4 = jnp.concatenate(
                [q_ref[pl.ds(r, A_BLOCK), (kvh * A_GROUP + g) * HEAD_DIM:(kvh * A_GROUP + g + 1) * HEAD_DIM]
                 for g in range(A_GROUP)], axis=0)
            kw = kbuf[pl.ds(r, A_KEYS), kvh * HEAD_DIM:(kvh + 1) * HEAD_DIM]
            t.append(_nt_dot(q4, kw) * qk_scale + nbias[kvh] + kbias)
        p, den = [], []
        for kvh in range(A_KV_HEADS):
            m = jnp.maximum(jnp.max(t[kvh], axis=-1, keepdims=True), sink[kvh])
            e = jnp.exp2(t[kvh] - m)
            den.append(jnp.sum(e, axis=-1, keepdims=True) + jnp.exp2(sink[kvh] - m))
            p.append(e.astype(BF16))
        for kvh in range(A_KV_HEADS):
            vw = vbuf[pl.ds(r, A_KEYS), kvh * HEAD_DIM:(kvh + 1) * HEAD_DIM]
            o = jnp.dot(p[kvh], vw, preferred_element_type=F32) / den[kvh]
            for g in range(A_GROUP):
                h = kvh * A_GROUP + g
                o_ref[pl.ds(r, A_BLOCK), h * HEAD_DIM:(h + 1) * HEAD_DIM] = (
                    o[g * A_BLOCK:(g + 1) * A_BLOCK].astype(o_ref.dtype))
        return carry
    lax.fori_loop(0, TILE // A_BLOCK, body, 0)


def _win_attn(proj, sink, seqs):
    t = proj.shape[0]
    nt = t // TILE
    per = TILE // A_BLOCK
    last = t // A_BLOCK - 1
    kblk = P_A_K // A_KV
    vblk = P_A_V // A_KV
    prev = lambda col: (lambda g: (jnp.maximum(g * per - 1, 0), col))
    cur = lambda col: (lambda g: (g, col))
    nxt = lambda col: (lambda g: (jnp.minimum((g + 1) * per, last), col))
    return pl.pallas_call(
        functools.partial(_win_kernel, seqs),
        grid=(nt,),
        in_specs=[
            pl.BlockSpec(memory_space=pltpu.SMEM),
            pl.BlockSpec((TILE, A_Q), lambda g: (g, P_A_Q // A_Q)),
            pl.BlockSpec((A_BLOCK, A_KV), prev(kblk)),
            pl.BlockSpec((TILE, A_KV), cur(kblk)),
            pl.BlockSpec((A_BLOCK, A_KV), nxt(kblk)),
            pl.BlockSpec((A_BLOCK, A_KV), prev(vblk)),
            pl.BlockSpec((TILE, A_KV), cur(vblk)),
            pl.BlockSpec((A_BLOCK, A_KV), nxt(vblk)),
        ],
        out_specs=pl.BlockSpec((TILE, A_Q), lambda g: (g, 0)),
        out_shape=jax.ShapeDtypeStruct((t, A_Q), BF16),
        scratch_shapes=[pltpu.VMEM((TILE + 2 * A_BLOCK, A_KV), BF16),
                        pltpu.VMEM((TILE + 2 * A_BLOCK, A_KV), BF16)],
        compiler_params=_cparams(("parallel",)),
        name="win_attn",
    )(sink, proj, proj, proj, proj, proj, proj, proj)


GLA_TILE = 1024
GLA_BLK = 256


def _tn_dot(a, b):
    return lax.dot_general(a, b, (((0,), (0,)), ((), ())), preferred_element_type=F32)


def _gla_prepare(dirs, bias_ref):
    row = lax.broadcasted_iota(jnp.int32, (GLA_BLK, GLA_BLK), 0)
    col = lax.broadcasted_iota(jnp.int32, (GLA_BLK, GLA_BLK), 1)
    same = (row // B_CHUNK) == (col // B_CHUNK)
    tris = [same & ((row >= col) if d[-1] else (row <= col)) for d in dirs]
    tris_bf = [t.astype(BF16) for t in tris]
    biases = [bias_ref[n:n + 1, :] for n in range(len(dirs))]

    def body(i, carry):
        r = pl.multiple_of(i * GLA_BLK, GLA_BLK)
        rows = pl.ds(r, GLA_BLK)
        his, los = [], []
        for (q_ref, k_ref, v_ref, gr_ref, w2_ref, qt_ref, kd_ref, dec_ref, oi_ref, _), bias in zip(dirs, biases):
            y = (jnp.dot(gr_ref[rows, :].astype(BF16), w2_ref[...], preferred_element_type=F32) + bias) * LOG2E
            lg = (jnp.minimum(y, 0.0) - jnp.log2(1.0 + jnp.exp2(-jnp.abs(y)))) * (1.0 / B_GATE_TAU)
            hi = lg.astype(BF16)
            his.append(hi)
            los.append((lg - hi.astype(F32)).astype(BF16))
        bs, tots = [], []
        for n in range(len(dirs)):
            b = (jnp.dot(tris_bf[n], his[n], preferred_element_type=F32)
                 + jnp.dot(tris_bf[n], los[n], preferred_element_type=F32))
            bs.append(b)
            bc = b.reshape(GLA_BLK // B_CHUNK, B_CHUNK, B_QK)
            tots.append(jnp.broadcast_to(jnp.min(bc, axis=1, keepdims=True), bc.shape).reshape(GLA_BLK, B_QK))
        qts, kts = [], []
        for n, (q_ref, k_ref, v_ref, gr_ref, w2_ref, qt_ref, kd_ref, dec_ref, oi_ref, _) in enumerate(dirs):
            b, tot = bs[n], tots[n]
            q = q_ref[rows, :].astype(F32) * (B_DK ** -0.5)
            k = k_ref[rows, :].astype(F32)
            qt = (q * jnp.exp2(b)).astype(BF16)
            qts.append(qt)
            kts.append((k * jnp.exp2(-b)).astype(BF16))
            qt_ref[rows, :] = qt
            kd_ref[rows, :] = (k * jnp.exp2(tot - b)).astype(BF16)
            dec_ref[rows, :] = jnp.exp2(tot)
        for h in range(B_HEADS):
            ks = slice(h * B_DK, (h + 1) * B_DK)
            vs = slice(h * B_DV, (h + 1) * B_DV)
            for n, d in enumerate(dirs):
                v_ref, oi_ref = d[2], d[8]
                a = jnp.where(tris[n], _nt_dot(qts[n][:, ks], kts[n][:, ks]), 0.0).astype(BF16)
                oi_ref[rows, vs] = jnp.dot(a, v_ref[rows, vs], preferred_element_type=F32)
        return carry
    lax.fori_loop(0, GLA_TILE // GLA_BLK, body, 0)


def _gla_scan_chunks(chunks):
    work = [(c, h) for h in range(B_HEADS) for c in chunks]
    sts = [c[7][h] for c, h in work]
    for (c, h), st in zip(work, sts):
        r, v_ref, qt_ref, kd_ref, dec_ref, oi_ref, o_ref, st_ref = c
        rows = pl.ds(r, B_CHUNK)
        ks = slice(h * B_DK, (h + 1) * B_DK)
        vs = slice(h * B_DV, (h + 1) * B_DV)
        o = oi_ref[rows, vs] + _nt_dot(qt_ref[rows, ks], st.astype(BF16))
        o_ref[rows, vs] = o.astype(o_ref.dtype)
    for (c, h), st in zip(work, sts):
        r, v_ref, qt_ref, kd_ref, dec_ref, oi_ref, o_ref, st_ref = c
        rows = pl.ds(r, B_CHUNK)
        ks = slice(h * B_DK, (h + 1) * B_DK)
        vs = slice(h * B_DV, (h + 1) * B_DV)
        st_ref[h] = st * dec_ref[pl.ds(r, 1), ks] + _tn_dot(v_ref[rows, vs], kd_ref[rows, ks])


def _gla_kernel(seqs, qf_ref, kf_ref, vf_ref, grf_ref, qb_ref, kb_ref, vb_ref, grb_ref, w2f_ref, w2b_ref,
                bias_ref, of_ref, ob_ref, sf_ref, sb_ref, qtf, kdf, decf, oif, qtb, kdb, decb, oib):
    i = pl.program_id(0)
    nt = pl.num_programs(0)
    f_start = i * GLA_TILE
    b_start = (nt - 1 - i) * GLA_TILE
    f_lo, _ = _seq_bounds(f_start, seqs)
    _, b_hi = _seq_bounds(b_start, seqs)

    @pl.when(f_start == f_lo)
    def _():
        sf_ref[...] = jnp.zeros_like(sf_ref)

    @pl.when(b_start + GLA_TILE == b_hi)
    def _():
        sb_ref[...] = jnp.zeros_like(sb_ref)

    _gla_prepare([(qf_ref, kf_ref, vf_ref, grf_ref, w2f_ref, qtf, kdf, decf, oif, True),
                  (qb_ref, kb_ref, vb_ref, grb_ref, w2b_ref, qtb, kdb, decb, oib, False)], bias_ref)

    nchunk = GLA_TILE // B_CHUNK

    def body(c, carry):
        rf = pl.multiple_of(c * B_CHUNK, B_CHUNK)
        rb = pl.multiple_of((nchunk - 1 - c) * B_CHUNK, B_CHUNK)
        _gla_scan_chunks([(rf, vf_ref, qtf, kdf, decf, oif, of_ref, sf_ref),
                          (rb, vb_ref, qtb, kdb, decb, oib, ob_ref, sb_ref)])
        return carry
    lax.fori_loop(0, nchunk, body, 0, unroll=4)


def _gla(proj, gr, w2f, w2b, bias, seqs):
    t = proj.shape[0]
    nt = t // GLA_TILE
    fwd = lambda col: (lambda i: (i, col))
    bwd = lambda col: (lambda i: (nt - 1 - i, col))
    qblk, kblk, vblk = P_B_Q // B_QK, P_B_K // B_QK, P_B_V // B_V

    def specs(mk):
        return [pl.BlockSpec((GLA_TILE, B_QK), mk(qblk)), pl.BlockSpec((GLA_TILE, B_QK), mk(kblk)),
                pl.BlockSpec((GLA_TILE, B_V), mk(vblk)), pl.BlockSpec((GLA_TILE, GR_PAD), mk(0))]

    const = lambda i: (0, 0)
    state = pltpu.VMEM((B_HEADS, B_DV, B_DK), F32)
    per_dir = [pltpu.VMEM((GLA_TILE, B_QK), BF16), pltpu.VMEM((GLA_TILE, B_QK), BF16),
               pltpu.VMEM((GLA_TILE, B_QK), F32), pltpu.VMEM((GLA_TILE, B_V), F32)]
    return pl.pallas_call(
        functools.partial(_gla_kernel, seqs),
        grid=(nt,),
        in_specs=specs(fwd) + specs(bwd) + [
            pl.BlockSpec((GR_PAD, B_QK), const), pl.BlockSpec((GR_PAD, B_QK), const),
            pl.BlockSpec((2, B_QK), const)],
        out_specs=[pl.BlockSpec((GLA_TILE, B_V), fwd(0)), pl.BlockSpec((GLA_TILE, B_V), bwd(0))],
        out_shape=[jax.ShapeDtypeStruct((t, B_V), BF16), jax.ShapeDtypeStruct((t, B_V), BF16)],
        scratch_shapes=[state, state] + per_dir + per_dir,
        compiler_params=_cparams(("arbitrary",)),
        name="gla",
    )(proj, proj, proj, gr, proj, proj, proj, gr, w2f, w2b, bias)


NA_PATTERNS = 3
_NA_ROWS_FOR_TABLE = 32
_NA_PATTERN_ROWS = (0, 2 * NA_Q_ROWS, _NA_ROWS_FOR_TABLE - NA_Q_ROWS)


def _na_bias_table(rpb):
    R = _NA_ROWS_FOR_TABLE
    kr = C_WIN_R
    key_rows = NA_KEY_ROWS
    rpb = rpb.astype(F32) * LOG2E
    n_dr = rpb.shape[1]
    edge = GRID_W - C_WIN_C
    ext = jnp.concatenate([jnp.repeat(rpb[:, :, :1], edge, axis=2), rpb, jnp.repeat(rpb[:, :, -1:], edge + 1, axis=2)], axis=2)
    skew = jnp.tile(ext, (1, 1, GRID_W))[:, :, :GRID_W * (2 * GRID_W - 1)].reshape(C_HEADS, n_dr, GRID_W, 2 * GRID_W - 1)
    col = skew[:, :, :, GRID_W - 1:]
    cq = np.arange(GRID_W)[:, None]
    ck = np.arange(GRID_W)[None, :]
    col_start = np.clip(cq - C_WIN_C // 2, 0, GRID_W - C_WIN_C)
    col_ok = (ck >= col_start) & (ck < col_start + C_WIN_C)
    col = jnp.where(col_ok[None, None], col, NEG_INF)
    colp = jnp.pad(col.transpose(0, 2, 1, 3), ((0, 0), (0, 0), (key_rows - kr, key_rows - kr), (0, 0)))
    colp = colp.reshape(C_HEADS, GRID_W, (n_dr + 2 * (key_rows - kr)) * GRID_W)
    pieces, row_ok = [], np.zeros((NA_PATTERNS * NA_Q_ROWS, 1, key_rows, GRID_W), bool)
    for r in _NA_PATTERN_ROWS:
        start = int(np.clip(r - kr // 2, 0, R - key_rows))
        for jj in range(NA_Q_ROWS):
            qrow = r + jj
            ws = int(np.clip(qrow - kr // 2, 0, R - kr))
            lead, a0 = ws - start, ws - qrow + (C_WIN_R - 1)
            assert 0 <= lead <= key_rows - kr and 0 <= a0 <= n_dr - kr
            row_ok[len(pieces), 0, lead:lead + kr] = True
            first = a0 - lead + key_rows - kr
            pieces.append(colp[:, :, first * GRID_W:(first + key_rows) * GRID_W])
    tab = jnp.stack(pieces, axis=1)
    tab = jnp.where(row_ok.reshape(1, NA_PATTERNS * NA_Q_ROWS, 1, NA_KEYS), tab, NEG_INF)
    return tab.reshape(C_HEADS, NA_PATTERNS, NA_Q, NA_KEYS)


def _na_kernel(seqs, q_ref, kp_ref, kc_ref, kn_ref, vp_ref, vc_ref, vn_ref, tab_ref, of_ref, obw_ref, og_ref,
               gain_ref, o_ref, ob_ref, kbuf, vbuf):
    tile_start = pl.program_id(1) * TILE
    seq_lo, seq_hi = _seq_bounds(tile_start, seqs)
    r0 = lax.div(tile_start, GRID_W)
    row_lo = lax.div(seq_lo, GRID_W)
    row_hi = lax.div(seq_hi, GRID_W)
    kbuf[0:NA_HALO, :] = kp_ref[...]
    kbuf[NA_HALO:NA_HALO + TILE, :] = kc_ref[...]
    kbuf[NA_HALO + TILE:, :] = kn_ref[...]
    vbuf[0:NA_HALO, :] = vp_ref[...]
    vbuf[NA_HALO:NA_HALO + TILE, :] = vc_ref[...]
    vbuf[NA_HALO + TILE:, :] = vn_ref[...]
    qk_scale = HEAD_DIM ** -0.5 * LOG2E

    trips = TILE // (NA_Q * NA_UNROLL)
    ep_rows = ob_ref.shape[0] // trips

    def body(it, carry):
        _gla_epilogue_rows(pl.ds(pl.multiple_of(it * ep_rows, ep_rows), ep_rows),
                           of_ref, obw_ref, og_ref, gain_ref, ob_ref)
        qs, off, pid = [], [], []
        for u in range(NA_UNROLL):
            grp = it * NA_UNROLL + u
            r = r0 + NA_Q_ROWS * grp
            start = jnp.clip(r - C_WIN_R // 2, row_lo, row_hi - NA_KEY_ROWS)
            off.append(pl.multiple_of((start - r0 + NA_HALO // GRID_W) * GRID_W, NA_Q))
            pid.append(jnp.where(r == row_lo, 0, jnp.where(r == row_hi - NA_Q_ROWS, 2, 1)))
            qs.append(pl.multiple_of(grp * NA_Q, NA_Q))
        t = [_nt_dot(q_ref[pl.ds(qs[u], NA_Q), :], kbuf[pl.ds(off[u], NA_KEYS), :]) * qk_scale + tab_ref[0, pid[u]]
             for u in range(NA_UNROLL)]
        p, den = [], []
        for u in range(NA_UNROLL):
            e = jnp.exp2(t[u] - jnp.max(t[u], axis=-1, keepdims=True))
            den.append(jnp.sum(e, axis=-1, keepdims=True))
            p.append(e.astype(BF16))
        for u in range(NA_UNROLL):
            o = jnp.dot(p[u], vbuf[pl.ds(off[u], NA_KEYS), :], preferred_element_type=F32) / den[u]
            o_ref[pl.ds(qs[u], NA_Q), :] = o.astype(o_ref.dtype)
        return carry
    lax.fori_loop(0, trips, body, 0)


def _na_attn(proj, table, o_f, o_bw, gain, seqs):
    t = proj.shape[0]
    nt = t // TILE
    per = TILE // NA_HALO
    last = t // NA_HALO - 1
    qblk, kblk, vblk = P_C_Q // HEAD_DIM, P_C_K // HEAD_DIM, P_C_V // HEAD_DIM
    prev = lambda c0: (lambda h, g: (jnp.maximum(g * per - 1, 0), c0 + h))
    cur = lambda c0: (lambda h, g: (g, c0 + h))
    nxt = lambda c0: (lambda h, g: (jnp.minimum((g + 1) * per, last), c0 + h))
    ep = TILE // C_HEADS
    ep_map = lambda c: (lambda h, g: (g * C_HEADS + h, c))
    return pl.pallas_call(
        functools.partial(_na_kernel, seqs),
        grid=(C_HEADS, nt),
        in_specs=[
            pl.BlockSpec((TILE, HEAD_DIM), cur(qblk)),
            pl.BlockSpec((NA_HALO, HEAD_DIM), prev(kblk)),
            pl.BlockSpec((TILE, HEAD_DIM), cur(kblk)),
            pl.BlockSpec((NA_HALO, HEAD_DIM), nxt(kblk)),
            pl.BlockSpec((NA_HALO, HEAD_DIM), prev(vblk)),
            pl.BlockSpec((TILE, HEAD_DIM), cur(vblk)),
            pl.BlockSpec((NA_HALO, HEAD_DIM), nxt(vblk)),
            pl.BlockSpec((1, NA_PATTERNS, NA_Q, NA_KEYS), lambda h, g: (h, 0, 0, 0)),
            pl.BlockSpec((ep, B_V), ep_map(0)),
            pl.BlockSpec((ep, B_V), ep_map(0)),
            pl.BlockSpec((ep, B_V), ep_map(P_B_OG // B_V)),
            pl.BlockSpec((1, B_DV), lambda h, g: (0, 0)),
        ],
        out_specs=[pl.BlockSpec((TILE, HEAD_DIM), lambda h, g: (g, h)), pl.BlockSpec((ep, B_V), ep_map(0))],
        out_shape=[jax.ShapeDtypeStruct((t, C_W), BF16), jax.ShapeDtypeStruct((t, B_V), BF16)],
        scratch_shapes=[pltpu.VMEM((TILE + 2 * NA_HALO, HEAD_DIM), BF16),
                        pltpu.VMEM((TILE + 2 * NA_HALO, HEAD_DIM), BF16)],
        compiler_params=_cparams(("parallel", "parallel")),
        name="na_attn",
    )(proj, proj, proj, proj, proj, proj, proj, table, o_f, o_bw, proj, gain)


def _pack_w_in(w):
    sl = lambda off, n: w[:, off:off + n]
    cols = [sl(IN_A_Q, A_Q), sl(IN_B_V, B_V), sl(IN_B_OG, B_V), sl(IN_C_Q, C_W), sl(IN_C_K, C_W), sl(IN_C_V, C_W),
            sl(IN_B_Q, B_QK), sl(IN_B_K, B_QK), sl(IN_A_K, A_KV), sl(IN_A_V, A_KV)]
    wp = jnp.concatenate(cols, axis=1).astype(BF16)
    wgr = jnp.pad(sl(IN_B_GR, 2 * B_GATE_RANK), ((0, 0), (0, GR_PAD - 2 * B_GATE_RANK))).astype(BF16)
    wgate = sl(IN_GL, 3 * D_MODEL).astype(BF16)
    return wp, wgr, wgate


def _pack_gla_w2(w2):
    w2f = jnp.pad(w2[0], ((0, GR_PAD - B_GATE_RANK), (0, 0))).astype(BF16)
    w2b = jnp.pad(w2[1], ((B_GATE_RANK, GR_PAD - 2 * B_GATE_RANK), (0, 0))).astype(BF16)
    return w2f, w2b


def kernel(x_prompt, x_sample, norm1, w_in, sink_a, gla_w2, gla_b, gla_norm, rpb_c, w_br_a, w_br_b, w_br_c, w_out,
           norm2, w_ffn_in, w_ffn_out, norm_f):
    bp, sp, d = x_prompt.shape
    bs, ss, _ = x_sample.shape
    tp, ts = bp * sp, bs * ss
    assert sp % TILE == 0 and ss % TILE == 0 and d == D_MODEL
    seqs = (tp, sp, ss)
    xs = [x_prompt.reshape(tp, d), x_sample.reshape(ts, d)]

    for l in range(DEPTH):
        wp, wgr, wgate = _pack_w_in(w_in[l])
        w2f, w2b = _pack_gla_w2(gla_w2[l])
        table = _na_bias_table(rpb_c[l])
        proj, gr, h = _inproj(xs, norm1[l].reshape(1, d), wp, wgr, bm=1024 if len(xs) == 1 else 512)
        o_a = _win_attn(proj, sink_a[l], seqs)
        o_f, o_bw = _gla(proj, gr, w2f, w2b, gla_b[l], seqs)
        o_c, o_b = _na_attn(proj, table, o_f, o_bw, gla_norm[l].reshape(1, B_DV), seqs)
        merged = _merge(o_a, o_b, o_c, h, wgate,
                        w_br_a[l].astype(BF16), w_br_b[l].astype(BF16), w_br_c[l].astype(BF16))
        x, h2 = _out_proj(merged, w_out[l].astype(BF16), xs, norm2[l].reshape(1, d))
        act = _ffn_in(h2, w_ffn_in[l].astype(BF16))
        if l + 1 < DEPTH:
            x = _mm_res(act, w_ffn_out[l].astype(BF16), x, 1024, 512, "ffn_out")
        xs = [x]

    gf = norm_f.reshape(1, d)
    wfo = w_ffn_out[DEPTH - 1].astype(BF16)
    y_p = _ffn_out_norm(act, wfo, x, gf, 0, tp).reshape(bp, sp, d)
    y_s = _ffn_out_norm(act, wfo, x, gf, tp, ts).reshape(bs, ss, d)
    return (y_p, y_s)
```

```python
import functools

import numpy as np
import jax
import jax.numpy as jnp
from jax import lax
from jax.experimental import pallas as pl
from jax.experimental.pallas import tpu as pltpu

F32 = jnp.float32
BF16 = jnp.bfloat16

D_MODEL = 2048
DEPTH = 2
GRID_W = 64
HEAD_DIM = 128
A_HEADS = 8
A_KV_HEADS = 2
A_GROUP = A_HEADS // A_KV_HEADS
A_WINDOW = 128
A_BLOCK = 128
B_HEADS = 4
B_DK = 128
B_DV = 256
B_CHUNK = 64
B_GATE_RANK = 16
B_GATE_TAU = 16.0
C_HEADS = 8
C_WIN_R = 8
C_WIN_C = 16
D_FF = -(-8 * D_MODEL // (3 * 256)) * 256
RMS_EPS = 1e-6
NEG_INF = -1e30
LOG2E = 1.4426950408889634

A_Q = A_HEADS * HEAD_DIM
A_KV = A_KV_HEADS * HEAD_DIM
B_QK = B_HEADS * B_DK
B_V = B_HEADS * B_DV
C_W = C_HEADS * HEAD_DIM
SPLIT_SIZES = (A_Q, A_KV, A_KV, B_QK, B_QK, B_V, B_V, 2 * B_GATE_RANK, C_W, C_W, C_W, 3 * D_MODEL)
SPLIT_OFF = tuple(int(i) for i in np.cumsum((0,) + SPLIT_SIZES))
(IN_A_Q, IN_A_K, IN_A_V, IN_B_Q, IN_B_K, IN_B_V, IN_B_OG, IN_B_GR, IN_C_Q, IN_C_K, IN_C_V, IN_GL) = SPLIT_OFF[:-1]

P_A_Q = 0
P_B_V = 1024
P_B_OG = 2048
P_C_Q = 3072
P_C_K = 4096
P_C_V = 5120
P_B_Q = 6144
P_B_K = 6656
P_A_K = 7168
P_A_V = 7424
P_COLS = 7680
GR_PAD = 128

TILE = 2048
NA_Q_ROWS = 4
NA_KEY_ROWS = 12
NA_Q = NA_Q_ROWS * GRID_W
NA_KEYS = NA_KEY_ROWS * GRID_W
NA_HALO = 256
NA_UNROLL = 4

VMEM_LIMIT = 56 * 1024 * 1024


def _cparams(sem):
    return pltpu.CompilerParams(dimension_semantics=sem, vmem_limit_bytes=VMEM_LIMIT)


def _seq_bounds(pos, seqs):
    tp, sp, ss = seqs
    in_p = pos < tp
    lo_p = lax.div(pos, sp) * sp
    lo_s = tp + lax.div(jnp.maximum(pos - tp, 0), ss) * ss
    lo = jnp.where(in_p, lo_p, lo_s)
    hi = lo + jnp.where(in_p, sp, ss)
    return lo, hi


NORM_ROWS = 256


def _rmsnorm_rows(x_ref, g_ref, h_ref):
    def body(c, carry):
        r = pl.multiple_of(c * NORM_ROWS, NORM_ROWS)
        xs = x_ref[pl.ds(r, NORM_ROWS), :]
        ms = jnp.mean(xs * xs, axis=-1, keepdims=True)
        h_ref[pl.ds(r, NORM_ROWS), :] = (xs * lax.rsqrt(ms + RMS_EPS) * g_ref[...]).astype(h_ref.dtype)
        return carry
    lax.fori_loop(0, x_ref.shape[0] // NORM_ROWS, body, 0)


def _row_tile_specs(xs, bm, d):
    if len(xs) == 1:
        return [pl.BlockSpec((bm, d), lambda i, *_: (i, 0))], None
    na = xs[0].shape[0] // bm
    return [pl.BlockSpec((bm, d), lambda i, *_: (jnp.minimum(i, na - 1), 0)),
            pl.BlockSpec((bm, d), lambda i, *_: (jnp.maximum(i - na, 0), 0))], na


def _inproj_kernel(na, *refs):
    nx = 1 if na is None else 2
    x_refs = refs[:nx]
    g_ref, w_ref, wgr_ref, proj_ref, gr_ref, h_ref = refs[nx:]

    @pl.when(pl.program_id(1) == 0)
    def _():
        if na is None:
            _rmsnorm_rows(x_refs[0], g_ref, h_ref)
        else:
            @pl.when(pl.program_id(0) < na)
            def _():
                _rmsnorm_rows(x_refs[0], g_ref, h_ref)

            @pl.when(pl.program_id(0) >= na)
            def _():
                _rmsnorm_rows(x_refs[1], g_ref, h_ref)
        gr_ref[...] = jnp.dot(h_ref[...], wgr_ref[...], preferred_element_type=F32)
    proj_ref[...] = jnp.dot(h_ref[...], w_ref[...], preferred_element_type=F32).astype(proj_ref.dtype)


def _inproj(xs, g, w, wgr, bm, bn=1536):
    t = sum(x.shape[0] for x in xs)
    d = xs[0].shape[1]
    n = w.shape[1]
    x_specs, na = _row_tile_specs(xs, bm, d)
    return pl.pallas_call(
        functools.partial(_inproj_kernel, na),
        grid=(t // bm, n // bn),
        in_specs=x_specs + [
            pl.BlockSpec((1, d), lambda i, j: (0, 0)),
            pl.BlockSpec((d, bn), lambda i, j: (0, j)),
            pl.BlockSpec((d, GR_PAD), lambda i, j: (0, 0)),
        ],
        out_specs=[
            pl.BlockSpec((bm, bn), lambda i, j: (i, j)),
            pl.BlockSpec((bm, GR_PAD), lambda i, j: (i, 0)),
            pl.BlockSpec((bm, d), lambda i, j: (i, 0)),
        ],
        out_shape=[jax.ShapeDtypeStruct((t, n), BF16), jax.ShapeDtypeStruct((t, GR_PAD), F32),
                   jax.ShapeDtypeStruct((t, d), BF16)],
        compiler_params=_cparams(("parallel", "arbitrary")),
        name="inproj",
    )(*xs, g, w, wgr)


def _ffn_in_kernel(h_ref, wg_ref, wu_ref, act_ref):
    h = h_ref[...]
    gate = jnp.dot(h, wg_ref[...], preferred_element_type=F32)
    up = jnp.dot(h, wu_ref[...], preferred_element_type=F32)
    act_ref[...] = (gate * jax.nn.sigmoid(gate) * up).astype(act_ref.dtype)


def _ffn_in(h, w, bm=1024, bn=512):
    t, d = h.shape
    ff = w.shape[1] // 2
    nj = ff // bn
    return pl.pallas_call(
        _ffn_in_kernel,
        grid=(t // bm, nj),
        in_specs=[
            pl.BlockSpec((bm, d), lambda i, j: (i, 0)),
            pl.BlockSpec((d, bn), lambda i, j: (0, j)),
            pl.BlockSpec((d, bn), lambda i, j: (0, nj + j)),
        ],
        out_specs=pl.BlockSpec((bm, bn), lambda i, j: (i, j)),
        out_shape=jax.ShapeDtypeStruct((t, ff), BF16),
        compiler_params=_cparams(("parallel", "arbitrary")),
        name="ffn_in",
    )(h, w, w)


def _out_proj_kernel(na, a_ref, w_ref, *refs):
    nx = 1 if na is None else 2
    x_refs = refs[:nx]
    g_ref, xo_ref, h_ref = refs[nx:]
    y = jnp.dot(a_ref[...], w_ref[...], preferred_element_type=F32)
    if na is None:
        xo_ref[...] = x_refs[0][...] + y
    else:
        @pl.when(pl.program_id(0) < na)
        def _():
            xo_ref[...] = x_refs[0][...] + y

        @pl.when(pl.program_id(0) >= na)
        def _():
            xo_ref[...] = x_refs[1][...] + y
    _rmsnorm_rows(xo_ref, g_ref, h_ref)


def _out_proj(a, w, xs, g, bm=512):
    t, k = a.shape
    d = w.shape[1]
    x_specs, na = _row_tile_specs(xs, bm, d)
    return pl.pallas_call(
        functools.partial(_out_proj_kernel, na),
        grid=(t // bm,),
        in_specs=[
            pl.BlockSpec((bm, k), lambda i: (i, 0)),
            pl.BlockSpec((k, d), lambda i: (0, 0)),
        ] + x_specs + [pl.BlockSpec((1, d), lambda i: (0, 0))],
        out_specs=[pl.BlockSpec((bm, d), lambda i: (i, 0)), pl.BlockSpec((bm, d), lambda i: (i, 0))],
        out_shape=[jax.ShapeDtypeStruct((t, d), F32), jax.ShapeDtypeStruct((t, d), BF16)],
        compiler_params=_cparams(("parallel",)),
        name="out_proj",
    )(a, w, *xs, g)


def _mm_res_kernel(a_ref, w_ref, x_ref, o_ref):
    o_ref[...] = x_ref[...] + jnp.dot(a_ref[...], w_ref[...], preferred_element_type=F32)


def _mm_res(a, w, x, bm, bn, name):
    t, k = a.shape
    n = w.shape[1]
    return pl.pallas_call(
        _mm_res_kernel,
        grid=(t // bm, n // bn),
        in_specs=[
            pl.BlockSpec((bm, k), lambda i, j: (i, 0)),
            pl.BlockSpec((k, bn), lambda i, j: (0, j)),
            pl.BlockSpec((bm, bn), lambda i, j: (i, j)),
        ],
        out_specs=pl.BlockSpec((bm, bn), lambda i, j: (i, j)),
        out_shape=jax.ShapeDtypeStruct((t, n), F32),
        compiler_params=_cparams(("parallel", "arbitrary")),
        name=name,
    )(a, w, x)


def _ffn_out_norm_kernel(a_ref, w_ref, x_ref, g_ref, y_ref):
    j = pl.program_id(1)
    bn = w_ref.shape[1]
    c0 = pl.multiple_of(j * bn, bn)
    y_ref[:, pl.ds(c0, bn)] = x_ref[...] + jnp.dot(a_ref[...], w_ref[...], preferred_element_type=F32)

    @pl.when(j == pl.num_programs(1) - 1)
    def _():
        _rmsnorm_rows(y_ref, g_ref, y_ref)


def _ffn_out_norm(a, w, x, g, row0, rows, bm=512, bn=512):
    k = a.shape[1]
    d = w.shape[1]
    blk0 = row0 // bm
    return pl.pallas_call(
        _ffn_out_norm_kernel,
        grid=(rows // bm, d // bn),
        in_specs=[
            pl.BlockSpec((bm, k), lambda i, j: (blk0 + i, 0)),
            pl.BlockSpec((k, bn), lambda i, j: (0, j)),
            pl.BlockSpec((bm, bn), lambda i, j: (blk0 + i, j)),
            pl.BlockSpec((1, d), lambda i, j: (0, 0)),
        ],
        out_specs=pl.BlockSpec((bm, d), lambda i, j: (i, 0)),
        out_shape=jax.ShapeDtypeStruct((rows, d), F32),
        compiler_params=_cparams(("parallel", "arbitrary")),
        name="ffn_out_norm",
    )(a, w, x, g)


def _gla_epilogue_rows(rows, of_ref, obw_ref, og_ref, gain_ref, ob_ref):
    o = of_ref[rows, :].astype(F32) + obw_ref[rows, :].astype(F32)
    og = og_ref[rows, :].astype(F32)
    for h in range(B_HEADS):
        sl = slice(h * B_DV, (h + 1) * B_DV)
        oh = o[:, sl]
        ms = jnp.mean(oh * oh, axis=-1, keepdims=True)
        ogh = og[:, sl]
        y = oh * lax.rsqrt(ms + RMS_EPS) * gain_ref[...] * (ogh * jax.nn.sigmoid(ogh))
        ob_ref[rows, sl] = y.astype(ob_ref.dtype)


def _merge_kernel(oa_ref, ob_ref, oc_ref, h_ref, wga_ref, wgb_ref, wgc_ref, wa_ref, wb_ref, wc_ref, out_ref):
    h = h_ref[...]
    ga = jax.nn.sigmoid(jnp.dot(h, wga_ref[...], preferred_element_type=F32))
    acc = ga * jnp.dot(oa_ref[...], wa_ref[...], preferred_element_type=F32)
    gb = jax.nn.sigmoid(jnp.dot(h, wgb_ref[...], preferred_element_type=F32))
    acc = acc + gb * jnp.dot(ob_ref[...], wb_ref[...], preferred_element_type=F32)
    gc = jax.nn.sigmoid(jnp.dot(h, wgc_ref[...], preferred_element_type=F32))
    acc = acc + gc * jnp.dot(oc_ref[...], wc_ref[...], preferred_element_type=F32)
    out_ref[...] = acc.astype(out_ref.dtype)


def _merge(o_a, o_b, o_c, h, wgate, wa, wb, wc, bm=1024, bn=512):
    t = o_a.shape[0]
    d = wa.shape[1]
    kb = o_a.shape[1]
    gstep = d // bn
    wide = lambda i, j: (i, 0)
    return pl.pallas_call(
        _merge_kernel,
        grid=(t // bm, d // bn),
        in_specs=[
            pl.BlockSpec((bm, kb), wide),
            pl.BlockSpec((bm, kb), wide),
            pl.BlockSpec((bm, kb), wide),
            pl.BlockSpec((bm, d), wide),
            pl.BlockSpec((d, bn), lambda i, j: (0, j)),
            pl.BlockSpec((d, bn), lambda i, j: (0, gstep + j)),
            pl.BlockSpec((d, bn), lambda i, j: (0, 2 * gstep + j)),
            pl.BlockSpec((kb, bn), lambda i, j: (0, j)),
            pl.BlockSpec((kb, bn), lambda i, j: (0, j)),
            pl.BlockSpec((kb, bn), lambda i, j: (0, j)),
        ],
        out_specs=pl.BlockSpec((bm, bn), lambda i, j: (i, j)),
        out_shape=jax.ShapeDtypeStruct((t, d), BF16),
        compiler_params=_cparams(("parallel", "arbitrary")),
        name="merge",
    )(o_a, o_b, o_c, h, wgate, wgate, wgate, wa, wb, wc)


A_KEYS = 3 * A_BLOCK
A_ROWS = A_GROUP * A_BLOCK


def _nt_dot(a, b):
    return lax.dot_general(a, b, (((1,), (1,)), ((), ())), preferred_element_type=F32)


def _win_kernel(seqs, sink_ref, q_ref, kp_ref, kc_ref, kn_ref, vp_ref, vc_ref, vn_ref, o_ref, kbuf, vbuf):
    tile_start = pl.program_id(0) * TILE
    seq_lo, seq_hi = _seq_bounds(tile_start, seqs)
    kbuf[0:A_BLOCK, :] = kp_ref[...]
    kbuf[A_BLOCK:A_BLOCK + TILE, :] = kc_ref[...]
    kbuf[A_BLOCK + TILE:, :] = kn_ref[...]
    vbuf[0:A_BLOCK, :] = vp_ref[...]
    vbuf[A_BLOCK:A_BLOCK + TILE, :] = vc_ref[...]
    vbuf[A_BLOCK + TILE:, :] = vn_ref[...]

    row = lax.broadcasted_iota(jnp.int32, (A_ROWS, A_KEYS), 0)
    col = lax.broadcasted_iota(jnp.int32, (A_ROWS, A_KEYS), 1)
    dist = jnp.abs((row % A_BLOCK) - (col - A_BLOCK))
    in_window = dist <= A_WINDOW
    distf = dist.astype(F32)
    grp = lax.broadcasted_iota(jnp.int32, (A_ROWS, 1), 0) // A_BLOCK
    kcol = lax.broadcasted_iota(jnp.int32, (1, A_KEYS), 1)
    qk_scale = HEAD_DIM ** -0.5 * LOG2E

    nbias, sink = [], []
    for kvh in range(A_KV_HEADS):
        slope = jnp.zeros((A_ROWS, 1), F32)
        snk = jnp.zeros((A_ROWS, 1), F32)
        for g in range(A_GROUP):
            h = kvh * A_GROUP + g
            slope = jnp.where(grp == g, 2.0 ** (-8.0 * (h + 1) / A_HEADS) * LOG2E, slope)
            snk = jnp.where(grp == g, sink_ref[h] * LOG2E, snk)
        nbias.append(jnp.where(in_window, -slope * distf, NEG_INF))
        sink.append(snk)

    def body(n, carry):
        r = pl.multiple_of(n * A_BLOCK, A_BLOCK)
        kabs = tile_start - A_BLOCK + r + kcol
        kbias = jnp.where((kabs >= seq_lo) & (kabs < seq_hi), 0.0, NEG_INF)
        t = []
        for kvh in range(A_KV_HEADS):
            q4 = jnp.concatenate(
                [q_ref[pl.ds(r, A_BLOCK), (kvh * A_GROUP + g) * HEAD_DIM:(kvh * A_GROUP + g + 1) * HEAD_DIM]
                 for g in range(A_GROUP)], axis=0)
            kw = kbuf[pl.ds(r, A_KEYS), kvh * HEAD_DIM:(kvh + 1) * HEAD_DIM]
            t.append(_nt_dot(q4, kw) * qk_scale + nbias[kvh] + kbias)
        p, den = [], []
        for kvh in range(A_KV_HEADS):
            m = jnp.maximum(jnp.max(t[kvh], axis=-1, keepdims=True), sink[kvh])
            e = jnp.exp2(t[kvh] - m)
            den.append(jnp.sum(e, axis=-1, keepdims=True) + jnp.exp2(sink[kvh] - m))
            p.append(e.astype(BF16))
        for kvh in range(A_KV_HEADS):
            vw = vbuf[pl.ds(r, A_KEYS), kvh * HEAD_DIM:(kvh + 1) * HEAD_DIM]
            o = jnp.dot(p[kvh], vw, preferred_element_type=F32) / den[kvh]
            for g in range(A_GROUP):
                h = kvh * A_GROUP + g
                o_ref[pl.ds(r, A_BLOCK), h * HEAD_DIM:(h + 1) * HEAD_DIM] = (
                    o[g * A_BLOCK:(g + 1) * A_BLOCK].astype(o_ref.dtype))
        return carry
    lax.fori_loop(0, TILE // A_BLOCK, body, 0)


def _win_attn(proj, sink, seqs):
    t = proj.shape[0]
    nt = t // TILE
    per = TILE // A_BLOCK
    last = t // A_BLOCK - 1
    kblk = P_A_K // A_KV
    vblk = P_A_V // A_KV
    prev = lambda col: (lambda g: (jnp.maximum(g * per - 1, 0), col))
    cur = lambda col: (lambda g: (g, col))
    nxt = lambda col: (lambda g: (jnp.minimum((g + 1) * per, last), col))
    return pl.pallas_call(
        functools.partial(_win_kernel, seqs),
        grid=(nt,),
        in_specs=[
            pl.BlockSpec(memory_space=pltpu.SMEM),
            pl.BlockSpec((TILE, A_Q), lambda g: (g, P_A_Q // A_Q)),
            pl.BlockSpec((A_BLOCK, A_KV), prev(kblk)),
            pl.BlockSpec((TILE, A_KV), cur(kblk)),
            pl.BlockSpec((A_BLOCK, A_KV), nxt(kblk)),
            pl.BlockSpec((A_BLOCK, A_KV), prev(vblk)),
            pl.BlockSpec((TILE, A_KV), cur(vblk)),
            pl.BlockSpec((A_BLOCK, A_KV), nxt(vblk)),
        ],
        out_specs=pl.BlockSpec((TILE, A_Q), lambda g: (g, 0)),
        out_shape=jax.ShapeDtypeStruct((t, A_Q), BF16),
        scratch_shapes=[pltpu.VMEM((TILE + 2 * A_BLOCK, A_KV), BF16),
                        pltpu.VMEM((TILE + 2 * A_BLOCK, A_KV), BF16)],
        compiler_params=_cparams(("parallel",)),
        name="win_attn",
    )(sink, proj, proj, proj, proj, proj, proj, proj)


GLA_TILE = 1024
GLA_BLK = 256


def _tn_dot(a, b):
    return lax.dot_general(a, b, (((0,), (0,)), ((), ())), preferred_element_type=F32)


def _gla_prepare(dirs, bias_ref):
    row = lax.broadcasted_iota(jnp.int32, (GLA_BLK, GLA_BLK), 0)
    col = lax.broadcasted_iota(jnp.int32, (GLA_BLK, GLA_BLK), 1)
    same = (row // B_CHUNK) == (col // B_CHUNK)
    tris = [same & ((row >= col) if d[-1] else (row <= col)) for d in dirs]
    tris_bf = [t.astype(BF16) for t in tris]
    biases = [bias_ref[n:n + 1, :] for n in range(len(dirs))]

    def body(i, carry):
        r = pl.multiple_of(i * GLA_BLK, GLA_BLK)
        rows = pl.ds(r, GLA_BLK)
        his, los = [], []
        for (q_ref, k_ref, v_ref, gr_ref, w2_ref, qt_ref, kd_ref, dec_ref, oi_ref, _), bias in zip(dirs, biases):
            y = (jnp.dot(gr_ref[rows, :].astype(BF16), w2_ref[...], preferred_element_type=F32) + bias) * LOG2E
            lg = (jnp.minimum(y, 0.0) - jnp.log2(1.0 + jnp.exp2(-jnp.abs(y)))) * (1.0 / B_GATE_TAU)
            hi = lg.astype(BF16)
            his.append(hi)
            los.append((lg - hi.astype(F32)).astype(BF16))
        bs, tots = [], []
        for n in range(len(dirs)):
            b = (jnp.dot(tris_bf[n], his[n], preferred_element_type=F32)
                 + jnp.dot(tris_bf[n], los[n], preferred_element_type=F32))
            bs.append(b)
            bc = b.reshape(GLA_BLK // B_CHUNK, B_CHUNK, B_QK)
            tots.append(jnp.broadcast_to(jnp.min(bc, axis=1, keepdims=True), bc.shape).reshape(GLA_BLK, B_QK))
        qts, kts = [], []
        for n, (q_ref, k_ref, v_ref, gr_ref, w2_ref, qt_ref, kd_ref, dec_ref, oi_ref, _) in enumerate(dirs):
            b, tot = bs[n], tots[n]
            q = q_ref[rows, :].astype(F32) * (B_DK ** -0.5)
            k = k_ref[rows, :].astype(F32)
            qt = (q * jnp.exp2(b)).astype(BF16)
            qts.append(qt)
            kts.append((k * jnp.exp2(-b)).astype(BF16))
            qt_ref[rows, :] = qt
            kd_ref[rows, :] = (k * jnp.exp2(tot - b)).astype(BF16)
            dec_ref[rows, :] = jnp.exp2(tot)
        for h in range(B_HEADS):
            ks = slice(h * B_DK, (h + 1) * B_DK)
            vs = slice(h * B_DV, (h + 1) * B_DV)
            for n, d in enumerate(dirs):
                v_ref, oi_ref = d[2], d[8]
                a = jnp.where(tris[n], _nt_dot(qts[n][:, ks], kts[n][:, ks]), 0.0).astype(BF16)
                oi_ref[rows, vs] = jnp.dot(a, v_ref[rows, vs], preferred_element_type=F32)
        return carry
    lax.fori_loop(0, GLA_TILE // GLA_BLK, body, 0)


def _gla_scan_chunks(chunks):
    work = [(c, h) for h in range(B_HEADS) for c in chunks]
    sts = [c[7][h] for c, h in work]
    for (c, h), st in zip(work, sts):
        r, v_ref, qt_ref, kd_ref, dec_ref, oi_ref, o_ref, st_ref = c
        rows = pl.ds(r, B_CHUNK)
        ks = slice(h * B_DK, (h + 1) * B_DK)
        vs = slice(h * B_DV, (h + 1) * B_DV)
        o = oi_ref[rows, vs] + _nt_dot(qt_ref[rows, ks], st.astype(BF16))
        o_ref[rows, vs] = o.astype(o_ref.dtype)
    for (c, h), st in zip(work, sts):
        r, v_ref, qt_ref, kd_ref, dec_ref, oi_ref, o_ref, st_ref = c
        rows = pl.ds(r, B_CHUNK)
        ks = slice(h * B_DK, (h + 1) * B_DK)
        vs = slice(h * B_DV, (h + 1) * B_DV)
        st_ref[h] = st * dec_ref[pl.ds(r, 1), ks] + _tn_dot(v_ref[rows, vs], kd_ref[rows, ks])


def _gla_kernel(seqs, qf_ref, kf_ref, vf_ref, grf_ref, qb_ref, kb_ref, vb_ref, grb_ref, w2f_ref, w2b_ref,
                bias_ref, of_ref, ob_ref, sf_ref, sb_ref, qtf, kdf, decf, oif, qtb, kdb, decb, oib):
    i = pl.program_id(0)
    nt = pl.num_programs(0)
    f_start = i * GLA_TILE
    b_start = (nt - 1 - i) * GLA_TILE
    f_lo, _ = _seq_bounds(f_start, seqs)
    _, b_hi = _seq_bounds(b_start, seqs)

    @pl.when(f_start == f_lo)
    def _():
        sf_ref[...] = jnp.zeros_like(sf_ref)

    @pl.when(b_start + GLA_TILE == b_hi)
    def _():
        sb_ref[...] = jnp.zeros_like(sb_ref)

    _gla_prepare([(qf_ref, kf_ref, vf_ref, grf_ref, w2f_ref, qtf, kdf, decf, oif, True),
                  (qb_ref, kb_ref, vb_ref, grb_ref, w2b_ref, qtb, kdb, decb, oib, False)], bias_ref)

    nchunk = GLA_TILE // B_CHUNK

    def body(c, carry):
        rf = pl.multiple_of(c * B_CHUNK, B_CHUNK)
        rb = pl.multiple_of((nchunk - 1 - c) * B_CHUNK, B_CHUNK)
        _gla_scan_chunks([(rf, vf_ref, qtf, kdf, decf, oif, of_ref, sf_ref),
                          (rb, vb_ref, qtb, kdb, decb, oib, ob_ref, sb_ref)])
        return carry
    lax.fori_loop(0, nchunk, body, 0, unroll=4)


def _gla(proj, gr, w2f, w2b, bias, seqs):
    t = proj.shape[0]
    nt = t // GLA_TILE
    fwd = lambda col: (lambda i: (i, col))
    bwd = lambda col: (lambda i: (nt - 1 - i, col))
    qblk, kblk, vblk = P_B_Q // B_QK, P_B_K // B_QK, P_B_V // B_V

    def specs(mk):
        return [pl.BlockSpec((GLA_TILE, B_QK), mk(qblk)), pl.BlockSpec((GLA_TILE, B_QK), mk(kblk)),
                pl.BlockSpec((GLA_TILE, B_V), mk(vblk)), pl.BlockSpec((GLA_TILE, GR_PAD), mk(0))]

    const = lambda i: (0, 0)
    state = pltpu.VMEM((B_HEADS, B_DV, B_DK), F32)
    per_dir = [pltpu.VMEM((GLA_TILE, B_QK), BF16), pltpu.VMEM((GLA_TILE, B_QK), BF16),
               pltpu.VMEM((GLA_TILE, B_QK), F32), pltpu.VMEM((GLA_TILE, B_V), F32)]
    return pl.pallas_call(
        functools.partial(_gla_kernel, seqs),
        grid=(nt,),
        in_specs=specs(fwd) + specs(bwd) + [
            pl.BlockSpec((GR_PAD, B_QK), const), pl.BlockSpec((GR_PAD, B_QK), const),
            pl.BlockSpec((2, B_QK), const)],
        out_specs=[pl.BlockSpec((GLA_TILE, B_V), fwd(0)), pl.BlockSpec((GLA_TILE, B_V), bwd(0))],
        out_shape=[jax.ShapeDtypeStruct((t, B_V), BF16), jax.ShapeDtypeStruct((t, B_V), BF16)],
        scratch_shapes=[state, state] + per_dir + per_dir,
        compiler_params=_cparams(("arbitrary",)),
        name="gla",
    )(proj, proj, proj, gr, proj, proj, proj, gr, w2f, w2b, bias)


NA_PATTERNS = 3
_NA_ROWS_FOR_TABLE = 32
_NA_PATTERN_ROWS = (0, 2 * NA_Q_ROWS, _NA_ROWS_FOR_TABLE - NA_Q_ROWS)


def _na_bias_table(rpb):
    R = _NA_ROWS_FOR_TABLE
    kr = C_WIN_R
    key_rows = NA_KEY_ROWS
    rpb = rpb.astype(F32) * LOG2E
    n_dr = rpb.shape[1]
    edge = GRID_W - C_WIN_C
    ext = jnp.concatenate([jnp.repeat(rpb[:, :, :1], edge, axis=2), rpb, jnp.repeat(rpb[:, :, -1:], edge + 1, axis=2)], axis=2)
    skew = jnp.tile(ext, (1, 1, GRID_W))[:, :, :GRID_W * (2 * GRID_W - 1)].reshape(C_HEADS, n_dr, GRID_W, 2 * GRID_W - 1)
    col = skew[:, :, :, GRID_W - 1:]
    cq = np.arange(GRID_W)[:, None]
    ck = np.arange(GRID_W)[None, :]
    col_start = np.clip(cq - C_WIN_C // 2, 0, GRID_W - C_WIN_C)
    col_ok = (ck >= col_start) & (ck < col_start + C_WIN_C)
    col = jnp.where(col_ok[None, None], col, NEG_INF)
    colp = jnp.pad(col.transpose(0, 2, 1, 3), ((0, 0), (0, 0), (key_rows - kr, key_rows - kr), (0, 0)))
    colp = colp.reshape(C_HEADS, GRID_W, (n_dr + 2 * (key_rows - kr)) * GRID_W)
    pieces, row_ok = [], np.zeros((NA_PATTERNS * NA_Q_ROWS, 1, key_rows, GRID_W), bool)
    for r in _NA_PATTERN_ROWS:
        start = int(np.clip(r - kr // 2, 0, R - key_rows))
        for jj in range(NA_Q_ROWS):
            qrow = r + jj
            ws = int(np.clip(qrow - kr // 2, 0, R - kr))
            lead, a0 = ws - start, ws - qrow + (C_WIN_R - 1)
            assert 0 <= lead <= key_rows - kr and 0 <= a0 <= n_dr - kr
            row_ok[len(pieces), 0, lead:lead + kr] = True
            first = a0 - lead + key_rows - kr
            pieces.append(colp[:, :, first * GRID_W:(first + key_rows) * GRID_W])
    tab = jnp.stack(pieces, axis=1)
    tab = jnp.where(row_ok.reshape(1, NA_PATTERNS * NA_Q_ROWS, 1, NA_KEYS), tab, NEG_INF)
    return tab.reshape(C_HEADS, NA_PATTERNS, NA_Q, NA_KEYS)


def _na_kernel(seqs, q_ref, kp_ref, kc_ref, kn_ref, vp_ref, vc_ref, vn_ref, tab_ref, of_ref, obw_ref, og_ref,
               gain_ref, o_ref, ob_ref, kbuf, vbuf):
    tile_start = pl.program_id(1) * TILE
    seq_lo, seq_hi = _seq_bounds(tile_start, seqs)
    r0 = lax.div(tile_start, GRID_W)
    row_lo = lax.div(seq_lo, GRID_W)
    row_hi = lax.div(seq_hi, GRID_W)
    kbuf[0:NA_HALO, :] = kp_ref[...]
    kbuf[NA_HALO:NA_HALO + TILE, :] = kc_ref[...]
    kbuf[NA_HALO + TILE:, :] = kn_ref[...]
    vbuf[0:NA_HALO, :] = vp_ref[...]
    vbuf[NA_HALO:NA_HALO + TILE, :] = vc_ref[...]
    vbuf[NA_HALO + TILE:, :] = vn_ref[...]
    qk_scale = HEAD_DIM ** -0.5 * LOG2E

    trips = TILE // (NA_Q * NA_UNROLL)
    ep_rows = ob_ref.shape[0] // trips

    def body(it, carry):
        _gla_epilogue_rows(pl.ds(pl.multiple_of(it * ep_rows, ep_rows), ep_rows),
                           of_ref, obw_ref, og_ref, gain_ref, ob_ref)
        qs, off, pid = [], [], []
        for u in range(NA_UNROLL):
            grp = it * NA_UNROLL + u
            r = r0 + NA_Q_ROWS * grp
            start = jnp.clip(r - C_WIN_R // 2, row_lo, row_hi - NA_KEY_ROWS)
            off.append(pl.multiple_of((start - r0 + NA_HALO // GRID_W) * GRID_W, NA_Q))
            pid.append(jnp.where(r == row_lo, 0, jnp.where(r == row_hi - NA_Q_ROWS, 2, 1)))
            qs.append(pl.multiple_of(grp * NA_Q, NA_Q))
        t = [_nt_dot(q_ref[pl.ds(qs[u], NA_Q), :], kbuf[pl.ds(off[u], NA_KEYS), :]) * qk_scale + tab_ref[0, pid[u]]
             for u in range(NA_UNROLL)]
        p, den = [], []
        for u in range(NA_UNROLL):
            e = jnp.exp2(t[u] - jnp.max(t[u], axis=-1, keepdims=True))
            den.append(jnp.sum(e, axis=-1, keepdims=True))
            p.append(e.astype(BF16))
        for u in range(NA_UNROLL):
            o = jnp.dot(p[u], vbuf[pl.ds(off[u], NA_KEYS), :], preferred_element_type=F32) / den[u]
            o_ref[pl.ds(qs[u], NA_Q), :] = o.astype(o_ref.dtype)
        return carry
    lax.fori_loop(0, trips, body, 0)


def _na_attn(proj, table, o_f, o_bw, gain, seqs):
    t = proj.shape[0]
    nt = t // TILE
    per = TILE // NA_HALO
    last = t // NA_HALO - 1
    qblk, kblk, vblk = P_C_Q // HEAD_DIM, P_C_K // HEAD_DIM, P_C_V // HEAD_DIM
    prev = lambda c0: (lambda h, g: (jnp.maximum(g * per - 1, 0), c0 + h))
    cur = lambda c0: (lambda h, g: (g, c0 + h))
    nxt = lambda c0: (lambda h, g: (jnp.minimum((g + 1) * per, last), c0 + h))
    ep = TILE // C_HEADS
    ep_map = lambda c: (lambda h, g: (g * C_HEADS + h, c))
    return pl.pallas_call(
        functools.partial(_na_kernel, seqs),
        grid=(C_HEADS, nt),
        in_specs=[
            pl.BlockSpec((TILE, HEAD_DIM), cur(qblk)),
            pl.BlockSpec((NA_HALO, HEAD_DIM), prev(kblk)),
            pl.BlockSpec((TILE, HEAD_DIM), cur(kblk)),
            pl.BlockSpec((NA_HALO, HEAD_DIM), nxt(kblk)),
            pl.BlockSpec((NA_HALO, HEAD_DIM), prev(vblk)),
            pl.BlockSpec((TILE, HEAD_DIM), cur(vblk)),
            pl.BlockSpec((NA_HALO, HEAD_DIM), nxt(vblk)),
            pl.BlockSpec((1, NA_PATTERNS, NA_Q, NA_KEYS), lambda h, g: (h, 0, 0, 0)),
            pl.BlockSpec((ep, B_V), ep_map(0)),
            pl.BlockSpec((ep, B_V), ep_map(0)),
            pl.BlockSpec((ep, B_V), ep_map(P_B_OG // B_V)),
            pl.BlockSpec((1, B_DV), lambda h, g: (0, 0)),
        ],
        out_specs=[pl.BlockSpec((TILE, HEAD_DIM), lambda h, g: (g, h)), pl.BlockSpec((ep, B_V), ep_map(0))],
        out_shape=[jax.ShapeDtypeStruct((t, C_W), BF16), jax.ShapeDtypeStruct((t, B_V), BF16)],
        scratch_shapes=[pltpu.VMEM((TILE + 2 * NA_HALO, HEAD_DIM), BF16),
                        pltpu.VMEM((TILE + 2 * NA_HALO, HEAD_DIM), BF16)],
        compiler_params=_cparams(("parallel", "parallel")),
        name="na_attn",
    )(proj, proj, proj, proj, proj, proj, proj, table, o_f, o_bw, proj, gain)


def _pack_w_in(w):
    sl = lambda off, n: w[:, off:off + n]
    cols = [sl(IN_A_Q, A_Q), sl(IN_B_V, B_V), sl(IN_B_OG, B_V), sl(IN_C_Q, C_W), sl(IN_C_K, C_W), sl(IN_C_V, C_W),
            sl(IN_B_Q, B_QK), sl(IN_B_K, B_QK), sl(IN_A_K, A_KV), sl(IN_A_V, A_KV)]
    wp = jnp.concatenate(cols, axis=1).astype(BF16)
    wgr = jnp.pad(sl(IN_B_GR, 2 * B_GATE_RANK), ((0, 0), (0, GR_PAD - 2 * B_GATE_RANK))).astype(BF16)
    wgate = sl(IN_GL, 3 * D_MODEL).astype(BF16)
    return wp, wgr, wgate


def _pack_gla_w2(w2):
    w2f = jnp.pad(w2[0], ((0, GR_PAD - B_GATE_RANK), (0, 0))).astype(BF16)
    w2b = jnp.pad(w2[1], ((B_GATE_RANK, GR_PAD - 2 * B_GATE_RANK), (0, 0))).astype(BF16)
    return w2f, w2b


def kernel(x_prompt, x_sample, norm1, w_in, sink_a, gla_w2, gla_b, gla_norm, rpb_c, w_br_a, w_br_b, w_br_c, w_out,
           norm2, w_ffn_in, w_ffn_out, norm_f):
    bp, sp, d = x_prompt.shape
    bs, ss, _ = x_sample.shape
    tp, ts = bp * sp, bs * ss
    assert sp % TILE == 0 and ss % TILE == 0 and d == D_MODEL
    seqs = (tp, sp, ss)
    xs = [x_prompt.reshape(tp, d), x_sample.reshape(ts, d)]

    for l in range(DEPTH):
        wp, wgr, wgate = _pack_w_in(w_in[l])
        w2f, w2b = _pack_gla_w2(gla_w2[l])
        table = _na_bias_table(rpb_c[l])
        proj, gr, h = _inproj(xs, norm1[l].reshape(1, d), wp, wgr, bm=1024, bn=1536 if len(xs) == 1 else 768)
        o_a = _win_attn(proj, sink_a[l], seqs)
        o_f, o_bw = _gla(proj, gr, w2f, w2b, gla_b[l], seqs)
        o_c, o_b = _na_attn(proj, table, o_f, o_bw, gla_norm[l].reshape(1, B_DV), seqs)
        merged = _merge(o_a, o_b, o_c, h, wgate,
                        w_br_a[l].astype(BF16), w_br_b[l].astype(BF16), w_br_c[l].astype(BF16))
        x, h2 = _out_proj(merged, w_out[l].astype(BF16), xs, norm2[l].reshape(1, d))
        act = _ffn_in(h2, w_ffn_in[l].astype(BF16))
        if l + 1 < DEPTH:
            x = _mm_res(act, w_ffn_out[l].astype(BF16), x, 1024, 512, "ffn_out")
        xs = [x]

    gf = norm_f.reshape(1, d)
    wfo = w_ffn_out[DEPTH - 1].astype(BF16)
    y_p = _ffn_out_norm(act, wfo, x, gf, 0, tp).reshape(bp, sp, d)
    y_s = _ffn_out_norm(act, wfo, x, gf, tp, ts).reshape(bs, ss, d)
    return (y_p, y_s)
```

```python
import functools

import numpy as np
import jax
import jax.numpy as jnp
from jax import lax
from jax.experimental import pallas as pl
from jax.experimental.pallas import tpu as pltpu

F32 = jnp.float32
BF16 = jnp.bfloat16

D_MODEL = 2048
DEPTH = 2
GRID_W = 64
HEAD_DIM = 128
A_HEADS = 8
A_KV_HEADS = 2
A_GROUP = A_HEADS // A_KV_HEADS
A_WINDOW = 128
A_BLOCK = 128
B_HEADS = 4
B_DK = 128
B_DV = 256
B_CHUNK = 64
B_GATE_RANK = 16
B_GATE_TAU = 16.0
C_HEADS = 8
C_WIN_R = 8
C_WIN_C = 16
D_FF = -(-8 * D_MODEL // (3 * 256)) * 256
RMS_EPS = 1e-6
NEG_INF = -1e30
LOG2E = 1.4426950408889634

A_Q = A_HEADS * HEAD_DIM
A_KV = A_KV_HEADS * HEAD_DIM
B_QK = B_HEADS * B_DK
B_V = B_HEADS * B_DV
C_W = C_HEADS * HEAD_DIM
SPLIT_SIZES = (A_Q, A_KV, A_KV, B_QK, B_QK, B_V, B_V, 2 * B_GATE_RANK, C_W, C_W, C_W, 3 * D_MODEL)
SPLIT_OFF = tuple(int(i) for i in np.cumsum((0,) + SPLIT_SIZES))
(IN_A_Q, IN_A_K, IN_A_V, IN_B_Q, IN_B_K, IN_B_V, IN_B_OG, IN_B_GR, IN_C_Q, IN_C_K, IN_C_V, IN_GL) = SPLIT_OFF[:-1]

P_A_Q = 0
P_B_V = 1024
P_B_OG = 2048
P_C_Q = 3072
P_C_K = 4096
P_C_V = 5120
P_B_Q = 6144
P_B_K = 6656
P_A_K = 7168
P_A_V = 7424
P_COLS = 7680
GR_PAD = 128

TILE = 2048
NA_Q_ROWS = 4
NA_KEY_ROWS = 12
NA_Q = NA_Q_ROWS * GRID_W
NA_KEYS = NA_KEY_ROWS * GRID_W
NA_HALO = 256
NA_UNROLL = 4

VMEM_LIMIT = 56 * 1024 * 1024


def _cparams(sem):
    return pltpu.CompilerParams(dimension_semantics=sem, vmem_limit_bytes=VMEM_LIMIT)


def _seq_bounds(pos, seqs):
    tp, sp, ss = seqs
    in_p = pos < tp
    lo_p = lax.div(pos, sp) * sp
    lo_s = tp + lax.div(jnp.maximum(pos - tp, 0), ss) * ss
    lo = jnp.where(in_p, lo_p, lo_s)
    hi = lo + jnp.where(in_p, sp, ss)
    return lo, hi


NORM_ROWS = 256


def _rmsnorm_rows(x_ref, g_ref, h_ref):
    def body(c, carry):
        r = pl.multiple_of(c * NORM_ROWS, NORM_ROWS)
        xs = x_ref[pl.ds(r, NORM_ROWS), :]
        ms = jnp.mean(xs * xs, axis=-1, keepdims=True)
        h_ref[pl.ds(r, NORM_ROWS), :] = (xs * lax.rsqrt(ms + RMS_EPS) * g_ref[...]).astype(h_ref.dtype)
        return carry
    lax.fori_loop(0, x_ref.shape[0] // NORM_ROWS, body, 0)


def _row_tile_specs(xs, bm, d):
    if len(xs) == 1:
        return [pl.BlockSpec((bm, d), lambda i, *_: (i, 0))], None
    na = xs[0].shape[0] // bm
    return [pl.BlockSpec((bm, d), lambda i, *_: (jnp.minimum(i, na - 1), 0)),
            pl.BlockSpec((bm, d), lambda i, *_: (jnp.maximum(i - na, 0), 0))], na


def _norm_groups_kernel(na, xa_ref, xb_ref, g_ref, h_ref):
    @pl.when(pl.program_id(0) < na)
    def _():
        _rmsnorm_rows(xa_ref, g_ref, h_ref)

    @pl.when(pl.program_id(0) >= na)
    def _():
        _rmsnorm_rows(xb_ref, g_ref, h_ref)


def _norm_groups(xs, g, bm=512):
    t = sum(x.shape[0] for x in xs)
    d = xs[0].shape[1]
    x_specs, na = _row_tile_specs(xs, bm, d)
    return pl.pallas_call(
        functools.partial(_norm_groups_kernel, na),
        grid=(t // bm,),
        in_specs=x_specs + [pl.BlockSpec((1, d), lambda i: (0, 0))],
        out_specs=pl.BlockSpec((bm, d), lambda i: (i, 0)),
        out_shape=jax.ShapeDtypeStruct((t, d), BF16),
        compiler_params=_cparams(("parallel",)),
        name="norm_groups",
    )(*xs, g)


def _inproj_kernel(fused_norm, *refs):
    if fused_norm:
        x_ref, g_ref, w_ref, wgr_ref, proj_ref, gr_ref, h_ref = refs
    else:
        h_ref, w_ref, wgr_ref, proj_ref, gr_ref = refs

    @pl.when(pl.program_id(1) == 0)
    def _():
        if fused_norm:
            _rmsnorm_rows(x_ref, g_ref, h_ref)
        gr_ref[...] = jnp.dot(h_ref[...], wgr_ref[...], preferred_element_type=F32)
    proj_ref[...] = jnp.dot(h_ref[...], w_ref[...], preferred_element_type=F32).astype(proj_ref.dtype)


def _inproj(x, g, w, wgr, bm=1024, bn=1536):
    t, d = x.shape
    n = w.shape[1]
    fused_norm = g is not None
    row_spec = pl.BlockSpec((bm, d), lambda i, j: (i, 0))
    w_specs = [pl.BlockSpec((d, bn), lambda i, j: (0, j)), pl.BlockSpec((d, GR_PAD), lambda i, j: (0, 0))]
    out_specs = [pl.BlockSpec((bm, bn), lambda i, j: (i, j)), pl.BlockSpec((bm, GR_PAD), lambda i, j: (i, 0))]
    out_shape = [jax.ShapeDtypeStruct((t, n), BF16), jax.ShapeDtypeStruct((t, GR_PAD), F32)]
    if fused_norm:
        in_specs = [row_spec, pl.BlockSpec((1, d), lambda i, j: (0, 0))] + w_specs
        out_specs, out_shape = out_specs + [row_spec], out_shape + [jax.ShapeDtypeStruct((t, d), BF16)]
        args = (x, g, w, wgr)
    else:
        in_specs, args = [row_spec] + w_specs, (x, w, wgr)
    outs = pl.pallas_call(
        functools.partial(_inproj_kernel, fused_norm),
        grid=(t // bm, n // bn),
        in_specs=in_specs,
        out_specs=out_specs,
        out_shape=out_shape,
        compiler_params=_cparams(("parallel", "arbitrary")),
        name="inproj",
    )(*args)
    return tuple(outs) if fused_norm else (outs[0], outs[1], x)


def _ffn_in_kernel(h_ref, wg_ref, wu_ref, act_ref):
    h = h_ref[...]
    gate = jnp.dot(h, wg_ref[...], preferred_element_type=F32)
    up = jnp.dot(h, wu_ref[...], preferred_element_type=F32)
    act_ref[...] = (gate * jax.nn.sigmoid(gate) * up).astype(act_ref.dtype)


def _ffn_in(h, w, bm=1024, bn=512):
    t, d = h.shape
    ff = w.shape[1] // 2
    nj = ff // bn
    return pl.pallas_call(
        _ffn_in_kernel,
        grid=(t // bm, nj),
        in_specs=[
            pl.BlockSpec((bm, d), lambda i, j: (i, 0)),
            pl.BlockSpec((d, bn), lambda i, j: (0, j)),
            pl.BlockSpec((d, bn), lambda i, j: (0, nj + j)),
        ],
        out_specs=pl.BlockSpec((bm, bn), lambda i, j: (i, j)),
        out_shape=jax.ShapeDtypeStruct((t, ff), BF16),
        compiler_params=_cparams(("parallel", "arbitrary")),
        name="ffn_in",
    )(h, w, w)


def _out_proj_kernel(na, a_ref, w_ref, *refs):
    nx = 1 if na is None else 2
    x_refs = refs[:nx]
    g_ref, xo_ref, h_ref = refs[nx:]
    y = jnp.dot(a_ref[...], w_ref[...], preferred_element_type=F32)
    if na is None:
        xo_ref[...] = x_refs[0][...] + y
    else:
        @pl.when(pl.program_id(0) < na)
        def _():
            xo_ref[...] = x_refs[0][...] + y

        @pl.when(pl.program_id(0) >= na)
        def _():
            xo_ref[...] = x_refs[1][...] + y
    _rmsnorm_rows(xo_ref, g_ref, h_ref)


def _out_proj(a, w, xs, g, bm=512):
    t, k = a.shape
    d = w.shape[1]
    x_specs, na = _row_tile_specs(xs, bm, d)
    return pl.pallas_call(
        functools.partial(_out_proj_kernel, na),
        grid=(t // bm,),
        in_specs=[
            pl.BlockSpec((bm, k), lambda i: (i, 0)),
            pl.BlockSpec((k, d), lambda i: (0, 0)),
        ] + x_specs + [pl.BlockSpec((1, d), lambda i: (0, 0))],
        out_specs=[pl.BlockSpec((bm, d), lambda i: (i, 0)), pl.BlockSpec((bm, d), lambda i: (i, 0))],
        out_shape=[jax.ShapeDtypeStruct((t, d), F32), jax.ShapeDtypeStruct((t, d), BF16)],
        compiler_params=_cparams(("parallel",)),
        name="out_proj",
    )(a, w, *xs, g)


def _mm_res_kernel(a_ref, w_ref, x_ref, o_ref):
    o_ref[...] = x_ref[...] + jnp.dot(a_ref[...], w_ref[...], preferred_element_type=F32)


def _mm_res(a, w, x, bm, bn, name):
    t, k = a.shape
    n = w.shape[1]
    return pl.pallas_call(
        _mm_res_kernel,
        grid=(t // bm, n // bn),
        in_specs=[
            pl.BlockSpec((bm, k), lambda i, j: (i, 0)),
            pl.BlockSpec((k, bn), lambda i, j: (0, j)),
            pl.BlockSpec((bm, bn), lambda i, j: (i, j)),
        ],
        out_specs=pl.BlockSpec((bm, bn), lambda i, j: (i, j)),
        out_shape=jax.ShapeDtypeStruct((t, n), F32),
        compiler_params=_cparams(("parallel", "arbitrary")),
        name=name,
    )(a, w, x)


def _ffn_out_norm_kernel(a_ref, w_ref, x_ref, g_ref, y_ref):
    j = pl.program_id(1)
    bn = w_ref.shape[1]
    c0 = pl.multiple_of(j * bn, bn)
    y_ref[:, pl.ds(c0, bn)] = x_ref[...] + jnp.dot(a_ref[...], w_ref[...], preferred_element_type=F32)

    @pl.when(j == pl.num_programs(1) - 1)
    def _():
        _rmsnorm_rows(y_ref, g_ref, y_ref)


def _ffn_out_norm(a, w, x, g, row0, rows, bm=512, bn=512):
    k = a.shape[1]
    d = w.shape[1]
    blk0 = row0 // bm
    return pl.pallas_call(
        _ffn_out_norm_kernel,
        grid=(rows // bm, d // bn),
        in_specs=[
            pl.BlockSpec((bm, k), lambda i, j: (blk0 + i, 0)),
            pl.BlockSpec((k, bn), lambda i, j: (0, j)),
            pl.BlockSpec((bm, bn), lambda i, j: (blk0 + i, j)),
            pl.BlockSpec((1, d), lambda i, j: (0, 0)),
        ],
        out_specs=pl.BlockSpec((bm, d), lambda i, j: (i, 0)),
        out_shape=jax.ShapeDtypeStruct((rows, d), F32),
        compiler_params=_cparams(("parallel", "arbitrary")),
        name="ffn_out_norm",
    )(a, w, x, g)


def _gla_epilogue_rows(rows, of_ref, obw_ref, og_ref, gain_ref, ob_ref):
    o = of_ref[rows, :].astype(F32) + obw_ref[rows, :].astype(F32)
    og = og_ref[rows, :].astype(F32)
    for h in range(B_HEADS):
        sl = slice(h * B_DV, (h + 1) * B_DV)
        oh = o[:, sl]
        ms = jnp.mean(oh * oh, axis=-1, keepdims=True)
        ogh = og[:, sl]
        y = oh * lax.rsqrt(ms + RMS_EPS) * gain_ref[...] * (ogh * jax.nn.sigmoid(ogh))
        ob_ref[rows, sl] = y.astype(ob_ref.dtype)


def _merge_kernel(oa_ref, ob_ref, oc_ref, h_ref, wga_ref, wgb_ref, wgc_ref, wa_ref, wb_ref, wc_ref, out_ref):
    h = h_ref[...]
    ga = jax.nn.sigmoid(jnp.dot(h, wga_ref[...], preferred_element_type=F32))
    acc = ga * jnp.dot(oa_ref[...], wa_ref[...], preferred_element_type=F32)
    gb = jax.nn.sigmoid(jnp.dot(h, wgb_ref[...], preferred_element_type=F32))
    acc = acc + gb * jnp.dot(ob_ref[...], wb_ref[...], preferred_element_type=F32)
    gc = jax.nn.sigmoid(jnp.dot(h, wgc_ref[...], preferred_element_type=F32))
    acc = acc + gc * jnp.dot(oc_ref[...], wc_ref[...], preferred_element_type=F32)
    out_ref[...] = acc.astype(out_ref.dtype)


def _merge(o_a, o_b, o_c, h, wgate, wa, wb, wc, bm=1024, bn=512):
    t = o_a.shape[0]
    d = wa.shape[1]
    kb = o_a.shape[1]
    gstep = d // bn
    wide = lambda i, j: (i, 0)
    return pl.pallas_call(
        _merge_kernel,
        grid=(t // bm, d // bn),
        in_specs=[
            pl.BlockSpec((bm, kb), wide),
            pl.BlockSpec((bm, kb), wide),
            pl.BlockSpec((bm, kb), wide),
            pl.BlockSpec((bm, d), wide),
            pl.BlockSpec((d, bn), lambda i, j: (0, j)),
            pl.BlockSpec((d, bn), lambda i, j: (0, gstep + j)),
            pl.BlockSpec((d, bn), lambda i, j: (0, 2 * gstep + j)),
            pl.BlockSpec((kb, bn), lambda i, j: (0, j)),
            pl.BlockSpec((kb, bn), lambda i, j: (0, j)),
            pl.BlockSpec((kb, bn), lambda i, j: (0, j)),
        ],
        out_specs=pl.BlockSpec((bm, bn), lambda i, j: (i, j)),
        out_shape=jax.ShapeDtypeStruct((t, d), BF16),
        compiler_params=_cparams(("parallel", "arbitrary")),
        name="merge",
    )(o_a, o_b, o_c, h, wgate, wgate, wgate, wa, wb, wc)


A_KEYS = 3 * A_BLOCK
A_ROWS = A_GROUP * A_BLOCK


def _nt_dot(a, b):
    return lax.dot_general(a, b, (((1,), (1,)), ((), ())), preferred_element_type=F32)


def _win_kernel(seqs, sink_ref, q_ref, kp_ref, kc_ref, kn_ref, vp_ref, vc_ref, vn_ref, o_ref, kbuf, vbuf):
    tile_start = pl.program_id(0) * TILE
    seq_lo, seq_hi = _seq_bounds(tile_start, seqs)
    kbuf[0:A_BLOCK, :] = kp_ref[...]
    kbuf[A_BLOCK:A_BLOCK + TILE, :] = kc_ref[...]
    kbuf[A_BLOCK + TILE:, :] = kn_ref[...]
    vbuf[0:A_BLOCK, :] = vp_ref[...]
    vbuf[A_BLOCK:A_BLOCK + TILE, :] = vc_ref[...]
    vbuf[A_BLOCK + TILE:, :] = vn_ref[...]

    row = lax.broadcasted_iota(jnp.int32, (A_ROWS, A_KEYS), 0)
    col = lax.broadcasted_iota(jnp.int32, (A_ROWS, A_KEYS), 1)
    dist = jnp.abs((row % A_BLOCK) - (col - A_BLOCK))
    in_window = dist <= A_WINDOW
    distf = dist.astype(F32)
    grp = lax.broadcasted_iota(jnp.int32, (A_ROWS, 1), 0) // A_BLOCK
    kcol = lax.broadcasted_iota(jnp.int32, (1, A_KEYS), 1)
    qk_scale = HEAD_DIM ** -0.5 * LOG2E

    nbias, sink = [], []
    for kvh in range(A_KV_HEADS):
        slope = jnp.zeros((A_ROWS, 1), F32)
        snk = jnp.zeros((A_ROWS, 1), F32)
        for g in range(A_GROUP):
            h = kvh * A_GROUP + g
            slope = jnp.where(grp == g, 2.0 ** (-8.0 * (h + 1) / A_HEADS) * LOG2E, slope)
            snk = jnp.where(grp == g, sink_ref[h] * LOG2E, snk)
        nbias.append(jnp.where(in_window, -slope * distf, NEG_INF))
        sink.append(snk)

    def body(n, carry):
        r = pl.multiple_of(n * A_BLOCK, A_BLOCK)
        kabs = tile_start - A_BLOCK + r + kcol
        kbias = jnp.where((kabs >= seq_lo) & (kabs < seq_hi), 0.0, NEG_INF)
        t = []
        for kvh in range(A_KV_HEADS):
            q4 = jnp.concatenate(
                [q_ref[pl.ds(r, A_BLOCK), (kvh * A_GROUP + g) * HEAD_DIM:(kvh * A_GROUP + g + 1) * HEAD_DIM]
                 for g in range(A_GROUP)], axis=0)
            kw = kbuf[pl.ds(r, A_KEYS), kvh * HEAD_DIM:(kvh + 1) * HEAD_DIM]
            t.append(_nt_dot(q4, kw) * qk_scale + nbias[kvh] + kbias)
        p, den = [], []
        for kvh in range(A_KV_HEADS):
            m = jnp.maximum(jnp.max(t[kvh], axis=-1, keepdims=True), sink[kvh])
            e = jnp.exp2(t[kvh] - m)
            den.append(jnp.sum(e, axis=-1, keepdims=True) + jnp.exp2(sink[kvh] - m))
            p.append(e.astype(BF16))
        for kvh in range(A_KV_HEADS):
            vw = vbuf[pl.ds(r, A_KEYS), kvh * HEAD_DIM:(kvh + 1) * HEAD_DIM]
            o = jnp.dot(p[kvh], vw, preferred_element_type=F32) / den[kvh]
            for g in range(A_GROUP):
                h = kvh * A_GROUP + g
                o_ref[pl.ds(r, A_BLOCK), h * HEAD_DIM:(h + 1) * HEAD_DIM] = (
                    o[g * A_BLOCK:(g + 1) * A_BLOCK].astype(o_ref.dtype))
        return carry
    lax.fori_loop(0, TILE // A_BLOCK, body, 0)


def _win_attn(proj, sink, seqs):
    t = proj.shape[0]
    nt = t // TILE
    per = TILE // A_BLOCK
    last = t // A_BLOCK - 1
    kblk = P_A_K // A_KV
    vblk = P_A_V // A_KV
    prev = lambda col: (lambda g: (jnp.maximum(g * per - 1, 0), col))
    cur = lambda col: (lambda g: (g, col))
    nxt = lambda col: (lambda g: (jnp.minimum((g + 1) * per, last), col))
    return pl.pallas_call(
        functools.partial(_win_kernel, seqs),
        grid=(nt,),
        in_specs=[
            pl.BlockSpec(memory_space=pltpu.SMEM),
            pl.BlockSpec((TILE, A_Q), lambda g: (g, P_A_Q // A_Q)),
            pl.BlockSpec((A_BLOCK, A_KV), prev(kblk)),
            pl.BlockSpec((TILE, A_KV), cur(kblk)),
            pl.BlockSpec((A_BLOCK, A_KV), nxt(kblk)),
            pl.BlockSpec((A_BLOCK, A_KV), prev(vblk)),
            pl.BlockSpec((TILE, A_KV), cur(vblk)),
            pl.BlockSpec((A_BLOCK, A_KV), nxt(vblk)),
        ],
        out_specs=pl.BlockSpec((TILE, A_Q), lambda g: (g, 0)),
        out_shape=jax.ShapeDtypeStruct((t, A_Q), BF16),
        scratch_shapes=[pltpu.VMEM((TILE + 2 * A_BLOCK, A_KV), BF16),
                        pltpu.VMEM((TILE + 2 * A_BLOCK, A_KV), BF16)],
        compiler_params=_cparams(("parallel",)),
        name="win_attn",
    )(sink, proj, proj, proj, proj, proj, proj, proj)


GLA_TILE = 1024
GLA_BLK = 256


def _tn_dot(a, b):
    return lax.dot_general(a, b, (((0,), (0,)), ((), ())), preferred_element_type=F32)


def _gla_prepare(dirs, bias_ref):
    row = lax.broadcasted_iota(jnp.int32, (GLA_BLK, GLA_BLK), 0)
    col = lax.broadcasted_iota(jnp.int32, (GLA_BLK, GLA_BLK), 1)
    same = (row // B_CHUNK) == (col // B_CHUNK)
    tris = [same & ((row >= col) if d[-1] else (row <= col)) for d in dirs]
    tris_bf = [t.astype(BF16) for t in tris]
    biases = [bias_ref[n:n + 1, :] for n in range(len(dirs))]

    def body(i, carry):
        r = pl.multiple_of(i * GLA_BLK, GLA_BLK)
        rows = pl.ds(r, GLA_BLK)
        his, los = [], []
        for (q_ref, k_ref, v_ref, gr_ref, w2_ref, qt_ref, kd_ref, dec_ref, oi_ref, _), bias in zip(dirs, biases):
            y = (jnp.dot(gr_ref[rows, :].astype(BF16), w2_ref[...], preferred_element_type=F32) + bias) * LOG2E
            lg = (jnp.minimum(y, 0.0) - jnp.log2(1.0 + jnp.exp2(-jnp.abs(y)))) * (1.0 / B_GATE_TAU)
            hi = lg.astype(BF16)
            his.append(hi)
            los.append((lg - hi.astype(F32)).astype(BF16))
        bs, tots = [], []
        for n in range(len(dirs)):
            b = (jnp.dot(tris_bf[n], his[n], preferred_element_type=F32)
                 + jnp.dot(tris_bf[n], los[n], preferred_element_type=F32))
            bs.append(b)
            bc = b.reshape(GLA_BLK // B_CHUNK, B_CHUNK, B_QK)
            tots.append(jnp.broadcast_to(jnp.min(bc, axis=1, keepdims=True), bc.shape).reshape(GLA_BLK, B_QK))
        qts, kts = [], []
        for n, (q_ref, k_ref, v_ref, gr_ref, w2_ref, qt_ref, kd_ref, dec_ref, oi_ref, _) in enumerate(dirs):
            b, tot = bs[n], tots[n]
            q = q_ref[rows, :].astype(F32) * (B_DK ** -0.5)
            k = k_ref[rows, :].astype(F32)
            qt = (q * jnp.exp2(b)).astype(BF16)
            qts.append(qt)
            kts.append((k * jnp.exp2(-b)).astype(BF16))
            qt_ref[rows, :] = qt
            kd_ref[rows, :] = (k * jnp.exp2(tot - b)).astype(BF16)
            dec_ref[rows, :] = jnp.exp2(tot)
        for h in range(B_HEADS):
            ks = slice(h * B_DK, (h + 1) * B_DK)
            vs = slice(h * B_DV, (h + 1) * B_DV)
            for n, d in enumerate(dirs):
                v_ref, oi_ref = d[2], d[8]
                a = jnp.where(tris[n], _nt_dot(qts[n][:, ks], kts[n][:, ks]), 0.0).astype(BF16)
                oi_ref[rows, vs] = jnp.dot(a, v_ref[rows, vs], preferred_element_type=F32)
        return carry
    lax.fori_loop(0, GLA_TILE // GLA_BLK, body, 0)


def _gla_scan_chunks(chunks):
    work = [(c, h) for h in range(B_HEADS) for c in chunks]
    sts = [c[7][h] for c, h in work]
    for (c, h), st in zip(work, sts):
        r, v_ref, qt_ref, kd_ref, dec_ref, oi_ref, o_ref, st_ref = c
        rows = pl.ds(r, B_CHUNK)
        ks = slice(h * B_DK, (h + 1) * B_DK)
        vs = slice(h * B_DV, (h + 1) * B_DV)
        o = oi_ref[rows, vs] + _nt_dot(qt_ref[rows, ks], st.astype(BF16))
        o_ref[rows, vs] = o.astype(o_ref.dtype)
    for (c, h), st in zip(work, sts):
        r, v_ref, qt_ref, kd_ref, dec_ref, oi_ref, o_ref, st_ref = c
        rows = pl.ds(r, B_CHUNK)
        ks = slice(h * B_DK, (h + 1) * B_DK)
        vs = slice(h * B_DV, (h + 1) * B_DV)
        st_ref[h] = st * dec_ref[pl.ds(r, 1), ks] + _tn_dot(v_ref[rows, vs], kd_ref[rows, ks])


def _gla_kernel(seqs, qf_ref, kf_ref, vf_ref, grf_ref, qb_ref, kb_ref, vb_ref, grb_ref, w2f_ref, w2b_ref,
                bias_ref, of_ref, ob_ref, sf_ref, sb_ref, qtf, kdf, decf, oif, qtb, kdb, decb, oib):
    i = pl.program_id(0)
    nt = pl.num_programs(0)
    f_start = i * GLA_TILE
    b_start = (nt - 1 - i) * GLA_TILE
    f_lo, _ = _seq_bounds(f_start, seqs)
    _, b_hi = _seq_bounds(b_start, seqs)

    @pl.when(f_start == f_lo)
    def _():
        sf_ref[...] = jnp.zeros_like(sf_ref)

    @pl.when(b_start + GLA_TILE == b_hi)
    def _():
        sb_ref[...] = jnp.zeros_like(sb_ref)

    _gla_prepare([(qf_ref, kf_ref, vf_ref, grf_ref, w2f_ref, qtf, kdf, decf, oif, True),
                  (qb_ref, kb_ref, vb_ref, grb_ref, w2b_ref, qtb, kdb, decb, oib, False)], bias_ref)

    nchunk = GLA_TILE // B_CHUNK

    def body(c, carry):
        rf = pl.multiple_of(c * B_CHUNK, B_CHUNK)
        rb = pl.multiple_of((nchunk - 1 - c) * B_CHUNK, B_CHUNK)
        _gla_scan_chunks([(rf, vf_ref, qtf, kdf, decf, oif, of_ref, sf_ref),
                          (rb, vb_ref, qtb, kdb, decb, oib, ob_ref, sb_ref)])
        return carry
    lax.fori_loop(0, nchunk, body, 0, unroll=4)


def _gla(proj, gr, w2f, w2b, bias, seqs):
    t = proj.shape[0]
    nt = t // GLA_TILE
    fwd = lambda col: (lambda i: (i, col))
    bwd = lambda col: (lambda i: (nt - 1 - i, col))
    qblk, kblk, vblk = P_B_Q // B_QK, P_B_K // B_QK, P_B_V // B_V

    def specs(mk):
        return [pl.BlockSpec((GLA_TILE, B_QK), mk(qblk)), pl.BlockSpec((GLA_TILE, B_QK), mk(kblk)),
                pl.BlockSpec((GLA_TILE, B_V), mk(vblk)), pl.BlockSpec((GLA_TILE, GR_PAD), mk(0))]

    const = lambda i: (0, 0)
    state = pltpu.VMEM((B_HEADS, B_DV, B_DK), F32)
    per_dir = [pltpu.VMEM((GLA_TILE, B_QK), BF16), pltpu.VMEM((GLA_TILE, B_QK), BF16),
               pltpu.VMEM((GLA_TILE, B_QK), F32), pltpu.VMEM((GLA_TILE, B_V), F32)]
    return pl.pallas_call(
        functools.partial(_gla_kernel, seqs),
        grid=(nt,),
        in_specs=specs(fwd) + specs(bwd) + [
            pl.BlockSpec((GR_PAD, B_QK), const), pl.BlockSpec((GR_PAD, B_QK), const),
            pl.BlockSpec((2, B_QK), const)],
        out_specs=[pl.BlockSpec((GLA_TILE, B_V), fwd(0)), pl.BlockSpec((GLA_TILE, B_V), bwd(0))],
        out_shape=[jax.ShapeDtypeStruct((t, B_V), BF16), jax.ShapeDtypeStruct((t, B_V), BF16)],
        scratch_shapes=[state, state] + per_dir + per_dir,
        compiler_params=_cparams(("arbitrary",)),
        name="gla",
    )(proj, proj, proj, gr, proj, proj, proj, gr, w2f, w2b, bias)


NA_PATTERNS = 3
_NA_ROWS_FOR_TABLE = 32
_NA_PATTERN_ROWS = (0, 2 * NA_Q_ROWS, _NA_ROWS_FOR_TABLE - NA_Q_ROWS)


def _na_bias_table(rpb):
    R = _NA_ROWS_FOR_TABLE
    kr = C_WIN_R
    key_rows = NA_KEY_ROWS
    rpb = rpb.astype(F32) * LOG2E
    n_dr = rpb.shape[1]
    edge = GRID_W - C_WIN_C
    ext = jnp.concatenate([jnp.repeat(rpb[:, :, :1], edge, axis=2), rpb, jnp.repeat(rpb[:, :, -1:], edge + 1, axis=2)], axis=2)
    skew = jnp.tile(ext, (1, 1, GRID_W))[:, :, :GRID_W * (2 * GRID_W - 1)].reshape(C_HEADS, n_dr, GRID_W, 2 * GRID_W - 1)
    col = skew[:, :, :, GRID_W - 1:]
    cq = np.arange(GRID_W)[:, None]
    ck = np.arange(GRID_W)[None, :]
    col_start = np.clip(cq - C_WIN_C // 2, 0, GRID_W - C_WIN_C)
    col_ok = (ck >= col_start) & (ck < col_start + C_WIN_C)
    col = jnp.where(col_ok[None, None], col, NEG_INF)
    colp = jnp.pad(col.transpose(0, 2, 1, 3), ((0, 0), (0, 0), (key_rows - kr, key_rows - kr), (0, 0)))
    colp = colp.reshape(C_HEADS, GRID_W, (n_dr + 2 * (key_rows - kr)) * GRID_W)
    pieces, row_ok = [], np.zeros((NA_PATTERNS * NA_Q_ROWS, 1, key_rows, GRID_W), bool)
    for r in _NA_PATTERN_ROWS:
        start = int(np.clip(r - kr // 2, 0, R - key_rows))
        for jj in range(NA_Q_ROWS):
            qrow = r + jj
            ws = int(np.clip(qrow - kr // 2, 0, R - kr))
            lead, a0 = ws - start, ws - qrow + (C_WIN_R - 1)
            assert 0 <= lead <= key_rows - kr and 0 <= a0 <= n_dr - kr
            row_ok[len(pieces), 0, lead:lead + kr] = True
            first = a0 - lead + key_rows - kr
            pieces.append(colp[:, :, first * GRID_W:(first + key_rows) * GRID_W])
    tab = jnp.stack(pieces, axis=1)
    tab = jnp.where(row_ok.reshape(1, NA_PATTERNS * NA_Q_ROWS, 1, NA_KEYS), tab, NEG_INF)
    return tab.reshape(C_HEADS, NA_PATTERNS, NA_Q, NA_KEYS)


def _na_kernel(seqs, q_ref, kp_ref, kc_ref, kn_ref, vp_ref, vc_ref, vn_ref, tab_ref, of_ref, obw_ref, og_ref,
               gain_ref, o_ref, ob_ref, kbuf, vbuf):
    tile_start = pl.program_id(1) * TILE
    seq_lo, seq_hi = _seq_bounds(tile_start, seqs)
    r0 = lax.div(tile_start, GRID_W)
    row_lo = lax.div(seq_lo, GRID_W)
    row_hi = lax.div(seq_hi, GRID_W)
    kbuf[0:NA_HALO, :] = kp_ref[...]
    kbuf[NA_HALO:NA_HALO + TILE, :] = kc_ref[...]
    kbuf[NA_HALO + TILE:, :] = kn_ref[...]
    vbuf[0:NA_HALO, :] = vp_ref[...]
    vbuf[NA_HALO:NA_HALO + TILE, :] = vc_ref[...]
    vbuf[NA_HALO + TILE:, :] = vn_ref[...]
    qk_scale = HEAD_DIM ** -0.5 * LOG2E

    trips = TILE // (NA_Q * NA_UNROLL)
    ep_rows = ob_ref.shape[0] // trips

    def body(it, carry):
        _gla_epilogue_rows(pl.ds(pl.multiple_of(it * ep_rows, ep_rows), ep_rows),
                           of_ref, obw_ref, og_ref, gain_ref, ob_ref)
        qs, off, pid = [], [], []
        for u in range(NA_UNROLL):
            grp = it * NA_UNROLL + u
            r = r0 + NA_Q_ROWS * grp
            start = jnp.clip(r - C_WIN_R // 2, row_lo, row_hi - NA_KEY_ROWS)
            off.append(pl.multiple_of((start - r0 + NA_HALO // GRID_W) * GRID_W, NA_Q))
            pid.append(jnp.where(r == row_lo, 0, jnp.where(r == row_hi - NA_Q_ROWS, 2, 1)))
            qs.append(pl.multiple_of(grp * NA_Q, NA_Q))
        t = [_nt_dot(q_ref[pl.ds(qs[u], NA_Q), :], kbuf[pl.ds(off[u], NA_KEYS), :]) * qk_scale + tab_ref[0, pid[u]]
             for u in range(NA_UNROLL)]
        p, den = [], []
        for u in range(NA_UNROLL):
            e = jnp.exp2(t[u] - jnp.max(t[u], axis=-1, keepdims=True))
            den.append(jnp.sum(e, axis=-1, keepdims=True))
            p.append(e.astype(BF16))
        for u in range(NA_UNROLL):
            o = jnp.dot(p[u], vbuf[pl.ds(off[u], NA_KEYS), :], preferred_element_type=F32) / den[u]
            o_ref[pl.ds(qs[u], NA_Q), :] = o.astype(o_ref.dtype)
        return carry
    lax.fori_loop(0, trips, body, 0)


def _na_attn(proj, table, o_f, o_bw, gain, seqs):
    t = proj.shape[0]
    nt = t // TILE
    per = TILE // NA_HALO
    last = t // NA_HALO - 1
    qblk, kblk, vblk = P_C_Q // HEAD_DIM, P_C_K // HEAD_DIM, P_C_V // HEAD_DIM
    prev = lambda c0: (lambda h, g: (jnp.maximum(g * per - 1, 0), c0 + h))
    cur = lambda c0: (lambda h, g: (g, c0 + h))
    nxt = lambda c0: (lambda h, g: (jnp.minimum((g + 1) * per, last), c0 + h))
    ep = TILE // C_HEADS
    ep_map = lambda c: (lambda h, g: (g * C_HEADS + h, c))
    return pl.pallas_call(
        functools.partial(_na_kernel, seqs),
        grid=(C_HEADS, nt),
        in_specs=[
            pl.BlockSpec((TILE, HEAD_DIM), cur(qblk)),
            pl.BlockSpec((NA_HALO, HEAD_DIM), prev(kblk)),
            pl.BlockSpec((TILE, HEAD_DIM), cur(kblk)),
            pl.BlockSpec((NA_HALO, HEAD_DIM), nxt(kblk)),
            pl.BlockSpec((NA_HALO, HEAD_DIM), prev(vblk)),
            pl.BlockSpec((TILE, HEAD_DIM), cur(vblk)),
            pl.BlockSpec((NA_HALO, HEAD_DIM), nxt(vblk)),
            pl.BlockSpec((1, NA_PATTERNS, NA_Q, NA_KEYS), lambda h, g: (h, 0, 0, 0)),
            pl.BlockSpec((ep, B_V), ep_map(0)),
            pl.BlockSpec((ep, B_V), ep_map(0)),
            pl.BlockSpec((ep, B_V), ep_map(P_B_OG // B_V)),
            pl.BlockSpec((1, B_DV), lambda h, g: (0, 0)),
        ],
        out_specs=[pl.BlockSpec((TILE, HEAD_DIM), lambda h, g: (g, h)), pl.BlockSpec((ep, B_V), ep_map(0))],
        out_shape=[jax.ShapeDtypeStruct((t, C_W), BF16), jax.ShapeDtypeStruct((t, B_V), BF16)],
        scratch_shapes=[pltpu.VMEM((TILE + 2 * NA_HALO, HEAD_DIM), BF16),
                        pltpu.VMEM((TILE + 2 * NA_HALO, HEAD_DIM), BF16)],
        compiler_params=_cparams(("parallel", "parallel")),
        name="na_attn",
    )(proj, proj, proj, proj, proj, proj, proj, table, o_f, o_bw, proj, gain)


def _pack_w_in(w):
    sl = lambda off, n: w[:, off:off + n]
    cols = [sl(IN_A_Q, A_Q), sl(IN_B_V, B_V), sl(IN_B_OG, B_V), sl(IN_C_Q, C_W), sl(IN_C_K, C_W), sl(IN_C_V, C_W),
            sl(IN_B_Q, B_QK), sl(IN_B_K, B_QK), sl(IN_A_K, A_KV), sl(IN_A_V, A_KV)]
    wp = jnp.concatenate(cols, axis=1).astype(BF16)
    wgr = jnp.pad(sl(IN_B_GR, 2 * B_GATE_RANK), ((0, 0), (0, GR_PAD - 2 * B_GATE_RANK))).astype(BF16)
    wgate = sl(IN_GL, 3 * D_MODEL).astype(BF16)
    return wp, wgr, wgate


def _pack_gla_w2(w2):
    w2f = jnp.pad(w2[0], ((0, GR_PAD - B_GATE_RANK), (0, 0))).astype(BF16)
    w2b = jnp.pad(w2[1], ((B_GATE_RANK, GR_PAD - 2 * B_GATE_RANK), (0, 0))).astype(BF16)
    return w2f, w2b


def kernel(x_prompt, x_sample, norm1, w_in, sink_a, gla_w2, gla_b, gla_norm, rpb_c, w_br_a, w_br_b, w_br_c, w_out,
           norm2, w_ffn_in, w_ffn_out, norm_f):
    bp, sp, d = x_prompt.shape
    bs, ss, _ = x_sample.shape
    tp, ts = bp * sp, bs * ss
    assert sp % TILE == 0 and ss % TILE == 0 and d == D_MODEL
    seqs = (tp, sp, ss)
    xs = [x_prompt.reshape(tp, d), x_sample.reshape(ts, d)]

    for l in range(DEPTH):
        wp, wgr, wgate = _pack_w_in(w_in[l])
        w2f, w2b = _pack_gla_w2(gla_w2[l])
        table = _na_bias_table(rpb_c[l])
        g1 = norm1[l].reshape(1, d)
        if len(xs) == 2:
            proj, gr, h = _inproj(_norm_groups(xs, g1), None, wp, wgr)
        else:
            proj, gr, h = _inproj(xs[0], g1, wp, wgr)
        o_a = _win_attn(proj, sink_a[l], seqs)
        o_f, o_bw = _gla(proj, gr, w2f, w2b, gla_b[l], seqs)
        o_c, o_b = _na_attn(proj, table, o_f, o_bw, gla_norm[l].reshape(1, B_DV), seqs)
        merged = _merge(o_a, o_b, o_c, h, wgate,
                        w_br_a[l].astype(BF16), w_br_b[l].astype(BF16), w_br_c[l].astype(BF16))
        x, h2 = _out_proj(merged, w_out[l].astype(BF16), xs, norm2[l].reshape(1, d))
        act = _ffn_in(h2, w_ffn_in[l].astype(BF16))
        if l + 1 < DEPTH:
            x = _mm_res(act, w_ffn_out[l].astype(BF16), x, 1024, 512, "ffn_out")
        xs = [x]

    gf = norm_f.reshape(1, d)
    wfo = w_ffn_out[DEPTH - 1].astype(BF16)
    y_p = _ffn_out_norm(act, wfo, x, gf, 0, tp).reshape(bp, sp, d)
    y_s = _ffn_out_norm(act, wfo, x, gf, tp, ts).reshape(bs, ss, d)
    return (y_p, y_s)
```
